```python
import jax, jax.numpy as jnp
from jax import lax
import numpy as np

D_MODEL = 4096
BATCH = 1
SEQ = 16384
DEPTH = 2

ATTN_HEADS = 16
ATTN_HEAD_DIM = 128
ATTN_WIDTH = ATTN_HEADS * ATTN_HEAD_DIM
Q_BLOCK = 128
CONV_CHANNELS = 1024
CONV_WIDTH = 31
HGRN_HEADS = 8
HGRN_HEAD_DIM = 128
HGRN_WIDTH = HGRN_HEADS * HGRN_HEAD_DIM
HGRN_CHUNK = 64
D_FF = 11008
FFN_CONV_WIDTH = 3
IN_SPLITS = (ATTN_WIDTH, ATTN_WIDTH, ATTN_WIDTH, ATTN_HEADS, 2 * CONV_CHANNELS,
             HGRN_WIDTH, HGRN_WIDTH, HGRN_WIDTH, HGRN_WIDTH, D_MODEL, D_MODEL, D_MODEL)
N_IN = sum(IN_SPLITS)
EPS = 1e-6
MASK_VALUE = -1e30
MIN_FORGET = 1e-30

kernel_name = 'hybrid_fox_conformer_hgrn2_block'


def rms_norm(x, g):
    xf = x.astype(jnp.float32)
    y = xf * lax.rsqrt(jnp.mean(xf * xf, axis=-1, keepdims=True) + EPS)
    return (y * g.astype(jnp.float32)).astype(x.dtype)


def layer_norm(x, g, b):
    xf = x.astype(jnp.float32)
    mu = jnp.mean(xf, axis=-1, keepdims=True)
    xc = xf - mu
    y = xc * lax.rsqrt(jnp.mean(xc * xc, axis=-1, keepdims=True) + EPS)
    return (y * g.astype(jnp.float32) + b.astype(jnp.float32)).astype(x.dtype)


def causal_depthwise_conv(x, w):
    k = w.shape[0]
    return lax.conv_general_dilated(
        x, w[:, None, :].astype(x.dtype), window_strides=(1,), padding=[(k - 1, 0)],
        dimension_numbers=('NWC', 'WIO', 'NWC'), feature_group_count=x.shape[-1])


def forgetting_attention(q, k, v, log_f):
    b, s, h, dh = q.shape
    nblk = s // Q_BLOCK
    qf = q.astype(jnp.float32) * (dh ** -0.5)
    kf = k.astype(jnp.float32)
    vf = v.astype(jnp.float32)
    c = jnp.cumsum(log_f.astype(jnp.float32), axis=1).transpose(0, 2, 1)
    q_blocks = qf.reshape(b, nblk, Q_BLOCK, h, dh).transpose(1, 0, 2, 3, 4)
    c_blocks = c.reshape(b, h, nblk, Q_BLOCK).transpose(2, 0, 1, 3)
    k_pos = jnp.arange(s)

    def one_block(args):
        q_blk, c_blk, blk = args
        q_pos = blk * Q_BLOCK + jnp.arange(Q_BLOCK)
        logits = jnp.einsum('bqhd,bkhd->bhqk', q_blk, kf) + (c_blk[..., :, None] - c[:, :, None, :])
        logits = jnp.where(k_pos[None, :] <= q_pos[:, None], logits, MASK_VALUE)
        p = jax.nn.softmax(logits, axis=-1)
        return jnp.einsum('bhqk,bkhd->bqhd', p, vf)

    out = lax.map(one_block, (q_blocks, c_blocks, jnp.arange(nblk)))
    return out.transpose(1, 0, 2, 3, 4).reshape(b, s, h * dh)


def hgrn2_recurrence(q, k, v, log_f):
    b, s, h, dk = q.shape
    dv = v.shape[-1]
    n = s // HGRN_CHUNK

    def to_chunks(t):
        return t.reshape(b, n, HGRN_CHUNK, h, t.shape[-1]).transpose(1, 0, 3, 2, 4)

    causal = jnp.tril(jnp.ones((HGRN_CHUNK, HGRN_CHUNK), dtype=bool))[:, :, None]

    def step(state, inp):
        qc, kc, vc, ac = inp
        cum = jnp.cumsum(ac, axis=2)
        o_inter = jnp.einsum('bhtd,bhde->bhte', qc * jnp.exp(cum), state)
        diff = cum[:, :, :, None, :] - cum[:, :, None, :, :]
        decay = jnp.where(causal, jnp.exp(jnp.where(causal, diff, 0.0)), 0.0)
        scores = jnp.einsum('bhtd,bhsd,bhtsd->bhts', qc, kc, decay)
        o_intra = jnp.einsum('bhts,bhse->bhte', scores, vc)
        cum_last = cum[:, :, -1:, :]
        new_state = (jnp.exp(cum_last[:, :, 0, :])[..., None] * state
                     + jnp.einsum('bhsd,bhse->bhde', kc * jnp.exp(cum_last - cum), vc))
        return new_state, o_inter + o_intra

    s0 = jnp.zeros((b, h, dk, dv), jnp.float32)
    _, o = lax.scan(step, s0, (to_chunks(q), to_chunks(k), to_chunks(v), to_chunks(log_f)))
    return o.transpose(1, 0, 3, 2, 4).reshape(b, s, h, dv)


def conv_ffn(h, w_up, w_dw, w_down):
    u = causal_depthwise_conv(h @ w_up, w_dw)
    gate, up = jnp.split(u, 2, axis=-1)
    return (jax.nn.silu(gate) * up) @ w_down


def setup_inputs(seed: int = 0) -> dict:
    key = jax.random.key(seed)
    ks = jax.random.split(key, 17)

    def normal(k, shape, scale):
        return jax.random.normal(k, shape, jnp.float32) * scale

    return {
        'x': normal(ks[0], (BATCH, SEQ, D_MODEL), 1.0),
        'norm_gains': 1.0 + normal(ks[1], (DEPTH, 4, D_MODEL), 0.02),
        'w_in': normal(ks[2], (DEPTH, D_MODEL, N_IN), D_MODEL ** -0.5),
        'b_fgate': normal(ks[3], (DEPTH, ATTN_HEADS), 0.1),
        'conv_dw': normal(ks[4], (DEPTH, CONV_WIDTH, CONV_CHANNELS), CONV_WIDTH ** -0.5),
        'conv_b': normal(ks[5], (DEPTH, CONV_CHANNELS), 0.02),
        'conv_ln_g': 1.0 + normal(ks[6], (DEPTH, CONV_CHANNELS), 0.02),
        'conv_ln_b': normal(ks[7], (DEPTH, CONV_CHANNELS), 0.02),
        'hgrn_lb_logits': normal(ks[8], (DEPTH, HGRN_WIDTH), 0.5),
        'hgrn_norm_g': 1.0 + normal(ks[9], (DEPTH, HGRN_WIDTH), 0.02),
        'w_attn_out': normal(ks[10], (DEPTH, ATTN_WIDTH, D_MODEL), ATTN_WIDTH ** -0.5),
        'w_conv_out': normal(ks[11], (DEPTH, CONV_CHANNELS, D_MODEL), CONV_CHANNELS ** -0.5),
        'w_hgrn_out': normal(ks[12], (DEPTH, HGRN_WIDTH, D_MODEL), HGRN_WIDTH ** -0.5),
        'w_o': normal(ks[13], (DEPTH, D_MODEL, D_MODEL), D_MODEL ** -0.5),
        'w_ffn_up': normal(ks[14], (DEPTH, D_MODEL, 2 * D_FF), D_MODEL ** -0.5),
        'ffn_dw': normal(ks[15], (DEPTH, FFN_CONV_WIDTH, 2 * D_FF), FFN_CONV_WIDTH ** -0.5),
        'w_ffn_down': normal(ks[16], (DEPTH, D_FF, D_MODEL), D_FF ** -0.5),
    }


def reference(x, norm_gains, w_in, b_fgate, conv_dw, conv_b, conv_ln_g, conv_ln_b,
              hgrn_lb_logits, hgrn_norm_g, w_attn_out, w_conv_out, w_hgrn_out, w_o,
              w_ffn_up, ffn_dw, w_ffn_down):
    b, s, _ = x.shape
    f32 = jnp.float32
    p_lb = jax.nn.softmax(hgrn_lb_logits.astype(f32), axis=0)
    lower_bounds = jnp.cumsum(p_lb, axis=0) - p_lb[0]
    offsets = np.cumsum(IN_SPLITS)[:-1].tolist()

    for l in range(DEPTH):
        h = rms_norm(x, norm_gains[l, 0])
        z = h @ w_in[l]
        (q, k, v, fg, glu, hq, hf, hi, hg, g_a, g_b, g_c) = jnp.split(z, offsets, axis=-1)

        log_f = jax.nn.log_sigmoid(fg.astype(f32) + b_fgate[l].astype(f32))
        att = forgetting_attention(q.reshape(b, s, ATTN_HEADS, ATTN_HEAD_DIM),
                                   k.reshape(b, s, ATTN_HEADS, ATTN_HEAD_DIM),
                                   v.reshape(b, s, ATTN_HEADS, ATTN_HEAD_DIM), log_f)
        y_a = att.astype(x.dtype) @ w_attn_out[l]

        val, gate = jnp.split(glu, 2, axis=-1)
        u = val * jax.nn.sigmoid(gate)
        u = causal_depthwise_conv(u, conv_dw[l]) + conv_b[l]
        u = jax.nn.silu(layer_norm(u, conv_ln_g[l], conv_ln_b[l]))
        y_b = u @ w_conv_out[l]

        lb = lower_bounds[l].reshape(HGRN_HEADS, HGRN_HEAD_DIM)
        zf = hf.astype(f32).reshape(b, s, HGRN_HEADS, HGRN_HEAD_DIM)
        forget = lb + (1.0 - lb) * jax.nn.sigmoid(zf)
        log_forget = jnp.log(jnp.maximum(forget, MIN_FORGET))
        key_in = (1.0 - lb) * jax.nn.sigmoid(-zf)
        query = jax.nn.silu(hq.astype(f32)).reshape(b, s, HGRN_HEADS, HGRN_HEAD_DIM) * (HGRN_HEAD_DIM ** -0.5)
        o = hgrn2_recurrence(query, key_in,
                             hi.astype(f32).reshape(b, s, HGRN_HEADS, HGRN_HEAD_DIM), log_forget)
        o = rms_norm(o, hgrn_norm_g[l].reshape(HGRN_HEADS, HGRN_HEAD_DIM)).reshape(b, s, HGRN_WIDTH)
        o = (o * jax.nn.silu(hg.astype(f32))).astype(x.dtype)
        y_c = o @ w_hgrn_out[l]

        merged = jax.nn.sigmoid(g_a) * y_a + jax.nn.sigmoid(g_b) * y_b + jax.nn.sigmoid(g_c) * y_c
        x = x + rms_norm(merged @ w_o[l], norm_gains[l, 1])

        h = rms_norm(x, norm_gains[l, 2])
        x = x + rms_norm(conv_ffn(h, w_ffn_up[l], ffn_dw[l], w_ffn_down[l]), norm_gains[l, 3])
    return x
```

```python
import functools

import jax
import jax.numpy as jnp
from jax import lax
from jax.experimental import pallas as pl
from jax.experimental.pallas import tpu as pltpu

F32 = jnp.float32
BF16 = jnp.bfloat16

HEAD_DIM = 128
LANES = 128
EPS = 1e-6
MASK_VALUE = -1e30
MIN_FORGET = 1e-30
HGRN_CHUNK = 128
V7X_VMEM_LIMIT_BYTES = 56 * 1024 * 1024


def _params(*semantics):
    return pltpu.CompilerParams(dimension_semantics=semantics, vmem_limit_bytes=V7X_VMEM_LIMIT_BYTES)


def _tile(n, preferred):
    t = preferred
    while t >= 8:
        if n % t == 0:
            return t
        t //= 2
    return n


def _rms(x, g):
    return x * lax.rsqrt(jnp.mean(x * x, axis=-1, keepdims=True) + EPS) * g


def _rmsnorm_kernel(x_ref, g_ref, h_ref):
    h_ref[...] = _rms(x_ref[...], g_ref[...]).astype(h_ref.dtype)


def _rmsnorm_bf16(x, g):
    s, d = x.shape
    tm = _tile(s, 512)
    return pl.pallas_call(
        _rmsnorm_kernel,
        out_shape=jax.ShapeDtypeStruct((s, d), BF16),
        grid=(s // tm,),
        in_specs=[pl.BlockSpec((tm, d), lambda i: (i, 0)), pl.BlockSpec((1, d), lambda i: (0, 0))],
        out_specs=pl.BlockSpec((tm, d), lambda i: (i, 0)),
        compiler_params=_params("parallel"),
        name="rmsnorm",
    )(x, g.reshape(1, d))


def _residual_norm_kernel(x_ref, y_ref, g_post_ref, g_next_ref, x_out_ref, h_ref):
    x_new = x_ref[...] + _rms(y_ref[...], g_post_ref[...])
    x_out_ref[...] = x_new
    h_ref[...] = _rms(x_new, g_next_ref[...]).astype(h_ref.dtype)


def _residual_norm(x, y, g_post, g_next):
    s, d = x.shape
    tm = _tile(s, 256)
    row = pl.BlockSpec((tm, d), lambda i: (i, 0))
    vec = pl.BlockSpec((1, d), lambda i: (0, 0))
    return pl.pallas_call(
        _residual_norm_kernel,
        out_shape=(jax.ShapeDtypeStruct((s, d), F32), jax.ShapeDtypeStruct((s, d), BF16)),
        grid=(s // tm,),
        in_specs=[row, row, vec, vec],
        out_specs=(row, row),
        compiler_params=_params("parallel"),
        name="residual_norm",
    )(x, y, g_post.reshape(1, d), g_next.reshape(1, d))


def _matmul_kernel(a_ref, b_ref, o_ref):
    o_ref[...] = jnp.dot(a_ref[...], b_ref[...], preferred_element_type=F32).astype(o_ref.dtype)


def _matmul(a, b, out_dtype, tm, tn, name):
    m, k = a.shape
    n = b.shape[1]
    tm, tn = _tile(m, tm), _tile(n, tn)
    return pl.pallas_call(
        _matmul_kernel,
        out_shape=jax.ShapeDtypeStruct((m, n), out_dtype),
        grid=(m // tm, n // tn),
        in_specs=[pl.BlockSpec((tm, k), lambda i, j: (i, 0)), pl.BlockSpec((k, tn), lambda i, j: (0, j))],
        out_specs=pl.BlockSpec((tm, tn), lambda i, j: (i, j)),
        compiler_params=_params("parallel", "parallel"),
        name=name,
    )(a, b)


def _lower_tri_bf16(t):
    r = lax.broadcasted_iota(jnp.int32, (t, t), 0)
    c = lax.broadcasted_iota(jnp.int32, (t, t), 1)
    return (c <= r).astype(BF16)


def _tri_cumsum(tri, x):
    hi = x.astype(BF16)
    r1 = x - hi.astype(F32)
    mid = r1.astype(BF16)
    lo = (r1 - mid.astype(F32)).astype(BF16)
    dot = functools.partial(jnp.dot, preferred_element_type=F32)
    return dot(tri, hi) + dot(tri, mid) + dot(tri, lo)


def _log_sigmoid(x):
    return jnp.minimum(x, 0.0) - jnp.log1p(jnp.exp(-jnp.abs(x)))


FGATE_SCAN_ROWS = 256


def _fgate_cumsum_kernel(fg_ref, b_ref, c_ref, carry_ref):
    @pl.when(pl.program_id(0) == 0)
    def _():
        carry_ref[...] = jnp.zeros_like(carry_ref)

    rows = fg_ref.shape[0]
    t = min(FGATE_SCAN_ROWS, rows)
    tri = _lower_tri_bf16(t)
    carry = carry_ref[...]
    for k in range(rows // t):
        ls = _log_sigmoid(fg_ref[k * t:(k + 1) * t, :] + b_ref[...])
        cs = _tri_cumsum(tri, ls) + carry
        c_ref[k * t:(k + 1) * t, :] = cs
        carry = cs[t - 1:t, :]
    carry_ref[...] = carry


def _fgate_cumsum(fg, bias_row):
    s, w = fg.shape
    tm = _tile(s, 2048)
    return pl.pallas_call(
        _fgate_cumsum_kernel,
        out_shape=jax.ShapeDtypeStruct((s, w), F32),
        grid=(s // tm,),
        in_specs=[pl.BlockSpec((tm, w), lambda i: (i, 0)), pl.BlockSpec((1, w), lambda i: (0, 0))],
        out_specs=pl.BlockSpec((tm, w), lambda i: (i, 0)),
        scratch_shapes=[pltpu.VMEM((1, w), F32)],
        compiler_params=_params("arbitrary"),
        name="fgate_cumsum",
    )(fg, bias_row)


def _attn_kernel(q_ref, k_ref, v_ref, c_ref, o_ref, m_ref, l_ref, acc_ref, *, tile):
    i = pl.program_id(1)
    q0 = pl.multiple_of(i * tile, tile)
    q = (q_ref[...].astype(F32) * (HEAD_DIM ** -0.5)).astype(BF16)
    c_base = c_ref[0, :, pl.ds(q0, LANES)][:, 0:1]

    m_ref[...] = jnp.full_like(m_ref, MASK_VALUE)
    l_ref[...] = jnp.zeros_like(l_ref)
    acc_ref[...] = jnp.zeros_like(acc_ref)

    def chunk(j, diagonal):
        k0 = pl.multiple_of(j * tile, tile)
        kj = k_ref[pl.ds(k0, tile), :]
        vj = v_ref[pl.ds(k0, tile), :]
        s = lax.dot_general(q, kj, (((1,), (1,)), ((), ())), preferred_element_type=F32)
        s = s + (c_base - c_ref[0, :, pl.ds(k0, tile)])
        if diagonal:
            row = lax.broadcasted_iota(jnp.int32, s.shape, 0)
            col = lax.broadcasted_iota(jnp.int32, s.shape, 1)
            s = jnp.where(col <= row, s, MASK_VALUE)
        m_prev = m_ref[...]
        m_new = jnp.maximum(m_prev, jnp.max(s, axis=1, keepdims=True))
        alpha = jnp.exp(m_prev - m_new)
        p = jnp.exp(s - m_new)
        l_ref[...] = alpha * l_ref[...] + jnp.sum(p, axis=1, keepdims=True)
        acc_ref[...] = alpha * acc_ref[...] + jnp.dot(p.astype(BF16), vj, preferred_element_type=F32)
        m_ref[...] = m_new

    def body(j, carry):
        chunk(j, False)
        return carry

    lax.fori_loop(0, i, body, 0)
    chunk(i, True)
    o_ref[...] = (acc_ref[...] / l_ref[...]).astype(o_ref.dtype)


def _attention(z, c_t, heads, q_blk, k_blk, v_blk):
    s = z.shape[0]
    tile = _tile(s, 1024)
    return pl.pallas_call(
        functools.partial(_attn_kernel, tile=tile),
        out_shape=jax.ShapeDtypeStruct((s, heads * HEAD_DIM), BF16),
        grid=(heads, s // tile),
        in_specs=[
            pl.BlockSpec((tile, HEAD_DIM), lambda h, i: (i, q_blk + h)),
            pl.BlockSpec((s, HEAD_DIM), lambda h, i: (0, k_blk + h)),
            pl.BlockSpec((s, HEAD_DIM), lambda h, i: (0, v_blk + h)),
            pl.BlockSpec((1, 1, s), lambda h, i: (h, 0, 0)),
        ],
        out_specs=pl.BlockSpec((tile, HEAD_DIM), lambda h, i: (i, h)),
        scratch_shapes=[pltpu.VMEM((tile, 1), F32), pltpu.VMEM((tile, 1), F32), pltpu.VMEM((tile, HEAD_DIM), F32)],
        compiler_params=_params("parallel", "parallel"),
        name="fox_attention",
    )(z, z, z, c_t)


CONV_HALO_ROWS = 32
CONV_ROW_CHUNK = 32


def _conv_kernel(val_ref, gate_ref, pval_ref, pgate_ref, w_ref, b_ref, g_ref, beta_ref, o_ref, u_ref, *, taps):
    tm = val_ref.shape[0]
    halo = pval_ref.shape[0]
    first = pl.program_id(0) == 0
    u_prev = pval_ref[...].astype(F32) * jax.nn.sigmoid(pgate_ref[...].astype(F32))
    u_ref[0:halo, :] = jnp.where(first, 0.0, u_prev)
    u_ref[halo:halo + tm, :] = val_ref[...].astype(F32) * jax.nn.sigmoid(gate_ref[...].astype(F32))

    rc = min(CONV_ROW_CHUNK, tm)
    for r in range(tm // rc):
        base = halo + r * rc - (taps - 1)
        acc = u_ref[base:base + rc, :] * w_ref[0:1, :]
        for k in range(1, taps):
            acc = acc + u_ref[base + k:base + k + rc, :] * w_ref[k:k + 1, :]
        acc = acc + b_ref[...]
        mu = jnp.mean(acc, axis=-1, keepdims=True)
        xc = acc - mu
        y = xc * lax.rsqrt(jnp.mean(xc * xc, axis=-1, keepdims=True) + EPS) * g_ref[...] + beta_ref[...]
        o_ref[r * rc:(r + 1) * rc, :] = (y * jax.nn.sigmoid(y)).astype(o_ref.dtype)


def _conv_module(z, glu_col, channels, w, b, ln_g, ln_b):
    s = z.shape[0]
    taps = w.shape[0]
    assert taps - 1 <= CONV_HALO_ROWS and glu_col % channels == 0
    tm = _tile(s, 128)
    assert tm % CONV_HALO_ROWS == 0
    vblk = glu_col // channels
    per = tm // CONV_HALO_ROWS
    cur = lambda off: pl.BlockSpec((tm, channels), lambda i: (i, vblk + off))
    prev = lambda off: pl.BlockSpec((CONV_HALO_ROWS, channels), lambda i: (jnp.maximum(i * per - 1, 0), vblk + off))
    vec = pl.BlockSpec((1, channels), lambda i: (0, 0))
    return pl.pallas_call(
        functools.partial(_conv_kernel, taps=taps),
        out_shape=jax.ShapeDtypeStruct((s, channels), BF16),
        grid=(s // tm,),
        in_specs=[cur(0), cur(1), prev(0), prev(1), pl.BlockSpec((taps, channels), lambda i: (0, 0)), vec, vec, vec],
        out_specs=pl.BlockSpec((tm, channels), lambda i: (i, 0)),
        scratch_shapes=[pltpu.VMEM((CONV_HALO_ROWS + tm, channels), F32)],
        compiler_params=_params("parallel"),
        name="conformer_conv",
    )(z, z, z, z, w, b.reshape(1, -1), ln_g.reshape(1, -1), ln_b.reshape(1, -1))


def _hgrn_kernel(q_ref, f_ref, i_ref, g_ref, lb_ref, gain_ref, o_ref, st_ref):
    @pl.when(pl.program_id(1) == 0)
    def _():
        st_ref[...] = jnp.zeros_like(st_ref)

    rows = q_ref.shape[0]
    c = min(HGRN_CHUNK, rows)
    levels = c.bit_length() - 1
    assert 1 << levels == c
    tri = _lower_tri_bf16(c)
    t_idx = lax.broadcasted_iota(jnp.int32, (c, c), 0)
    s_idx = lax.broadcasted_iota(jnp.int32, (c, c), 1)
    x = t_idx ^ s_idx
    split_level = jnp.full((c, c), -1, jnp.int32)
    for lvl in range(levels):
        split_level = split_level + (x >= (1 << lvl)).astype(jnp.int32)
    row_idx = lax.broadcasted_iota(jnp.int32, (c, HEAD_DIM), 0)
    lb = lb_ref[...]
    gain = gain_ref[...]
    nt = (((1,), (1,)), ((), ()))
    tn = (((0,), (0,)), ((), ()))

    def step(ci, carry):
        r0 = pl.multiple_of(ci * c, c)
        zf = f_ref[pl.ds(r0, c), :].astype(F32)
        forget = lb + (1.0 - lb) * jax.nn.sigmoid(zf)
        log_f = jnp.log(jnp.maximum(forget, MIN_FORGET))
        key = (1.0 - lb) * jax.nn.sigmoid(-zf)
        hq = q_ref[pl.ds(r0, c), :].astype(F32)
        query = hq * jax.nn.sigmoid(hq) * (HEAD_DIM ** -0.5)
        val = i_ref[pl.ds(r0, c), :]
        cum = _tri_cumsum(tri, log_f)

        st = st_ref[...]
        o = lax.dot_general((query * jnp.exp(cum)).astype(BF16), st.astype(BF16), nt, preferred_element_type=F32)

        scores = jnp.zeros((c, c), F32)
        seg_end = cum
        for lvl in range(levels):
            h = 1 << lvl
            odd = ((row_idx >> lvl) & 1) == 1
            prev_end = pltpu.roll(seg_end, h, axis=0)
            q_l = jnp.where(odd, query * jnp.exp(cum - prev_end), 0.0)
            k_l = jnp.where(odd, 0.0, key * jnp.exp(seg_end - cum))
            part = lax.dot_general(q_l.astype(BF16), k_l.astype(BF16), nt, preferred_element_type=F32)
            scores = scores + jnp.where(split_level == lvl, part, 0.0)
            seg_end = jnp.where(odd, seg_end, pltpu.roll(seg_end, c - h, axis=0))
        o = o + jnp.dot(scores.astype(BF16), val, preferred_element_type=F32)
        o = o + jnp.sum(query * key, axis=1, keepdims=True) * val.astype(F32)

        cum_last = cum[c - 1:c, :]
        k_dec = (key * jnp.exp(cum_last - cum)).astype(BF16)
        st_ref[...] = st * jnp.exp(cum_last) + lax.dot_general(val, k_dec, tn, preferred_element_type=F32)

        o = _rms(o, gain)
        hg = g_ref[pl.ds(r0, c), :].astype(F32)
        o_ref[pl.ds(r0, c), :] = (o * (hg * jax.nn.sigmoid(hg))).astype(o_ref.dtype)
        return carry

    lax.fori_loop(0, rows // c, step, 0)


def _hgrn(z, heads, q_blk, f_blk, i_blk, g_blk, lower_bound, gain):
    s = z.shape[0]
    tm = _tile(s, 1024)
    col = lambda blk: pl.BlockSpec((tm, HEAD_DIM), lambda h, i: (i, blk + h))
    vec = pl.BlockSpec((1, HEAD_DIM), lambda h, i: (0, h))
    return pl.pallas_call(
        _hgrn_kernel,
        out_shape=jax.ShapeDtypeStruct((s, heads * HEAD_DIM), BF16),
        grid=(heads, s // tm),
        in_specs=[col(q_blk), col(f_blk), col(i_blk), col(g_blk), vec, vec],
        out_specs=pl.BlockSpec((tm, HEAD_DIM), lambda h, i: (i, h)),
        scratch_shapes=[pltpu.VMEM((HEAD_DIM, HEAD_DIM), F32)],
        compiler_params=_params("parallel", "arbitrary"),
        name="hgrn2",
    )(z, z, z, z, lower_bound.reshape(1, -1), gain.reshape(1, -1))


def _merge_kernel(a_ref, b_ref, c_ref, wa_ref, wb_ref, wc_ref, ga_ref, gb_ref, gc_ref, o_ref):
    dot = functools.partial(jnp.dot, preferred_element_type=F32)
    gate = lambda ref: jax.nn.sigmoid(ref[...].astype(F32))
    merged = gate(ga_ref) * dot(a_ref[...], wa_ref[...])
    merged = merged + gate(gb_ref) * dot(b_ref[...], wb_ref[...])
    merged = merged + gate(gc_ref) * dot(c_ref[...], wc_ref[...])
    o_ref[...] = merged.astype(o_ref.dtype)


def _merge(att, conv, hgrn, wa, wb, wc, z, gate_col):
    s = att.shape[0]
    d = wa.shape[1]
    tm, tn = _tile(s, 1024), _tile(d, 512)
    assert gate_col % tn == 0 and d % tn == 0
    act = lambda a: pl.BlockSpec((tm, a.shape[1]), lambda i, j: (i, 0))
    wgt = lambda w: pl.BlockSpec((w.shape[0], tn), lambda i, j: (0, j))
    gate = lambda n: pl.BlockSpec((tm, tn), lambda i, j: (i, (gate_col + n * d) // tn + j))
    return pl.pallas_call(
        _merge_kernel,
        out_shape=jax.ShapeDtypeStruct((s, d), BF16),
        grid=(s // tm, d // tn),
        in_specs=[act(att), act(conv), act(hgrn), wgt(wa), wgt(wb), wgt(wc), gate(0), gate(1), gate(2)],
        out_specs=pl.BlockSpec((tm, tn), lambda i, j: (i, j)),
        compiler_params=_params("parallel", "parallel"),
        name="branch_merge",
    )(att, conv, hgrn, wa, wb, wc, z, z, z)


FFN_HALO_ROWS = 8


def _ffn_up_kernel(h_ref, hp_ref, wg_ref, wu_ref, dwg_ref, dwu_ref, o_ref, ug_ref, uu_ref, *, taps):
    tm = h_ref.shape[0]
    halo = hp_ref.shape[0]
    first = pl.program_id(0) == 0
    dot = functools.partial(jnp.dot, preferred_element_type=F32)

    def conv(w_ref, dw_ref, u_ref):
        u_ref[0:halo, :] = jnp.where(first, 0.0, dot(hp_ref[...], w_ref[...]))
        u_ref[halo:halo + tm, :] = dot(h_ref[...], w_ref[...])
        base = halo - (taps - 1)
        acc = u_ref[base:base + tm, :] * dw_ref[0:1, :]
        for k in range(1, taps):
            acc = acc + u_ref[base + k:base + k + tm, :] * dw_ref[k:k + 1, :]
        return acc

    gate = conv(wg_ref, dwg_ref, ug_ref)
    up = conv(wu_ref, dwu_ref, uu_ref)
    o_ref[...] = (gate * jax.nn.sigmoid(gate) * up).astype(o_ref.dtype)


def _ffn_up(h, w_up, dw, d_ff):
    s, d = h.shape
    taps = dw.shape[0]
    assert taps - 1 <= FFN_HALO_ROWS
    tm, tn = _tile(s, 1024), _tile(d_ff, 256)
    nj = d_ff // tn
    per = tm // FFN_HALO_ROWS
    return pl.pallas_call(
        functools.partial(_ffn_up_kernel, taps=taps),
        out_shape=jax.ShapeDtypeStruct((s, d_ff), BF16),
        grid=(s // tm, nj),
        in_specs=[
            pl.BlockSpec((tm, d), lambda i, j: (i, 0)),
            pl.BlockSpec((FFN_HALO_ROWS, d), lambda i, j: (jnp.maximum(i * per - 1, 0), 0)),
            pl.BlockSpec((d, tn), lambda i, j: (0, j)),
            pl.BlockSpec((d, tn), lambda i, j: (0, nj + j)),
            pl.BlockSpec((taps, tn), lambda i, j: (0, j)),
            pl.BlockSpec((taps, tn), lambda i, j: (0, nj + j)),
        ],
        out_specs=pl.BlockSpec((tm, tn), lambda i, j: (i, j)),
        scratch_shapes=[pltpu.VMEM((FFN_HALO_ROWS + tm, tn), F32), pltpu.VMEM((FFN_HALO_ROWS + tm, tn), F32)],
        compiler_params=_params("parallel", "parallel"),
        name="ffn_up_conv_glu",
    )(h, h, w_up, w_up, dw, dw)


def _layer(x, h, p, g_next):
    s, d = x.shape
    heads = p["b_fgate"].shape[0]
    attn_w = heads * HEAD_DIM
    conv_c = p["conv_b"].shape[0]
    hgrn_w = p["hgrn_norm_g"].shape[0]
    hgrn_heads = hgrn_w // HEAD_DIM
    d_ff = p["w_ffn_down"].shape[0]

    w_in = p["w_in"]
    fg0 = 3 * attn_w
    w_main = jnp.concatenate([w_in[:, :fg0], w_in[:, fg0 + heads:]], axis=1).astype(BF16)
    w_fg = jnp.pad(w_in[:, fg0:fg0 + heads], ((0, 0), (0, LANES - heads))).astype(BF16)
    z = _matmul(h, w_main, BF16, 1024, 1024, "in_proj")
    fg = _matmul(h, w_fg, F32, 1024, LANES, "in_proj_fgate")

    glu_col = 3 * attn_w
    hgrn_col = glu_col + 2 * conv_c
    gate_col = hgrn_col + 4 * hgrn_w
    blk = lambda col: col // HEAD_DIM

    bias_row = jnp.pad(p["b_fgate"].astype(F32), (0, LANES - heads)).reshape(1, LANES)
    c = _fgate_cumsum(fg, bias_row)
    c_t = c[:, :heads].T.reshape(heads, 1, s)
    att = _attention(z, c_t, heads, 0, blk(attn_w), blk(2 * attn_w))

    conv = _conv_module(z, glu_col, conv_c, p["conv_dw"], p["conv_b"], p["conv_ln_g"], p["conv_ln_b"])

    hg = _hgrn(z, hgrn_heads, blk(hgrn_col), blk(hgrn_col + hgrn_w), blk(hgrn_col + 2 * hgrn_w),
               blk(hgrn_col + 3 * hgrn_w), p["lower_bound"], p["hgrn_norm_g"])

    merged = _merge(att, conv, hg, p["w_attn_out"].astype(BF16), p["w_conv_out"].astype(BF16),
                    p["w_hgrn_out"].astype(BF16), z, gate_col)
    y = _matmul(merged, p["w_o"].astype(BF16), F32, 1024, 1024, "out_proj")
    x, h = _residual_norm(x, y, p["norm_gains"][1], p["norm_gains"][2])

    act = _ffn_up(h, p["w_ffn_up"].astype(BF16), p["ffn_dw"], d_ff)
    y = _matmul(act, p["w_ffn_down"].astype(BF16), F32, 512, 512, "ffn_down")
    return _residual_norm(x, y, p["norm_gains"][3], g_next)


def kernel(x, norm_gains, w_in, b_fgate, conv_dw, conv_b, conv_ln_g, conv_ln_b, hgrn_lb_logits, hgrn_norm_g,
           w_attn_out, w_conv_out, w_hgrn_out, w_o, w_ffn_up, ffn_dw, w_ffn_down):
    b, s, d = x.shape
    assert b == 1, "the sequence scans assume a single sequence"
    depth = w_in.shape[0]
    p_lb = jax.nn.softmax(hgrn_lb_logits.astype(F32), axis=0)
    lower_bounds = jnp.cumsum(p_lb, axis=0) - p_lb[0]

    xs = x.reshape(s, d)
    h = _rmsnorm_bf16(xs, norm_gains[0, 0])
    for l in range(depth):
        p = dict(norm_gains=norm_gains[l], w_in=w_in[l], b_fgate=b_fgate[l], conv_dw=conv_dw[l], conv_b=conv_b[l],
                 conv_ln_g=conv_ln_g[l], conv_ln_b=conv_ln_b[l], lower_bound=lower_bounds[l],
                 hgrn_norm_g=hgrn_norm_g[l], w_attn_out=w_attn_out[l], w_conv_out=w_conv_out[l],
                 w_hgrn_out=w_hgrn_out[l], w_o=w_o[l], w_ffn_up=w_ffn_up[l], ffn_dw=ffn_dw[l],
                 w_ffn_down=w_ffn_down[l])
        g_next = norm_gains[l + 1, 0] if l + 1 < depth else norm_gains[l, 0]
        xs, h = _layer(xs, h, p, g_next)
    return xs.reshape(b, s, d)
```

```python
import functools

import jax
import jax.numpy as jnp
from jax import lax
from jax.experimental import pallas as pl
from jax.experimental.pallas import tpu as pltpu

F32 = jnp.float32
BF16 = jnp.bfloat16

HEAD_DIM = 128
LANES = 128
SUBLANES = 8
EPS = 1e-6
MASK_VALUE = -1e30
MIN_FORGET = 1e-30
LOG2E = 1.4426950408889634
HGRN_CHUNK = 128
V7X_VMEM_LIMIT_BYTES = 56 * 1024 * 1024

NT_DIMS = (((1,), (1,)), ((), ()))
TN_DIMS = (((0,), (0,)), ((), ()))

ROWS = dict(rmsnorm=512, residual=256, fgate=2048, attention=1024, conv=128, hgrn=1024, merge=1024, ffn_up=1024)


def _params(*semantics):
    return pltpu.CompilerParams(dimension_semantics=semantics, vmem_limit_bytes=V7X_VMEM_LIMIT_BYTES)


def _tile(n, preferred):
    t = preferred
    while t >= 8:
        if n % t == 0:
            return t
        t //= 2
    return n


_dot = functools.partial(jnp.dot, preferred_element_type=F32)


def _rms(x, g):
    return x * lax.rsqrt(jnp.mean(x * x, axis=-1, keepdims=True) + EPS) * g


def _rmsnorm_kernel(x_ref, g_ref, h_ref):
    h_ref[...] = _rms(x_ref[...], g_ref[...]).astype(h_ref.dtype)


def _rmsnorm_bf16(x, g):
    s, d = x.shape
    tm = _tile(s, ROWS["rmsnorm"])
    return pl.pallas_call(
        _rmsnorm_kernel,
        out_shape=jax.ShapeDtypeStruct((s, d), BF16),
        grid=(s // tm,),
        in_specs=[pl.BlockSpec((tm, d), lambda i: (i, 0)), pl.BlockSpec((1, d), lambda i: (0, 0))],
        out_specs=pl.BlockSpec((tm, d), lambda i: (i, 0)),
        compiler_params=_params("parallel"),
        name="rmsnorm",
    )(x, g.reshape(1, d))


def _residual_norm_kernel(x_ref, y_ref, g_post_ref, g_next_ref, x_out_ref, h_ref):
    x_new = x_ref[...] + _rms(y_ref[...], g_post_ref[...])
    x_out_ref[...] = x_new
    h_ref[...] = _rms(x_new, g_next_ref[...]).astype(h_ref.dtype)


def _residual_kernel(x_ref, y_ref, g_post_ref, x_out_ref):
    x_out_ref[...] = x_ref[...] + _rms(y_ref[...], g_post_ref[...])


def _residual_norm(x, y, g_post, g_next):
    s, d = x.shape
    tm = _tile(s, ROWS["residual"])
    row = pl.BlockSpec((tm, d), lambda i: (i, 0))
    vec = pl.BlockSpec((1, d), lambda i: (0, 0))
    x_shape = jax.ShapeDtypeStruct((s, d), F32)
    if g_next is None:
        return pl.pallas_call(
            _residual_kernel, out_shape=x_shape, grid=(s // tm,), in_specs=[row, row, vec], out_specs=row,
            compiler_params=_params("parallel"), name="residual",
        )(x, y, g_post.reshape(1, d)), None
    return pl.pallas_call(
        _residual_norm_kernel,
        out_shape=(x_shape, jax.ShapeDtypeStruct((s, d), BF16)),
        grid=(s // tm,),
        in_specs=[row, row, vec, vec],
        out_specs=(row, row),
        compiler_params=_params("parallel"),
        name="residual_norm",
    )(x, y, g_post.reshape(1, d), g_next.reshape(1, d))


def _matmul_kernel(a_ref, b_ref, o_ref):
    o_ref[...] = _dot(a_ref[...], b_ref[...]).astype(o_ref.dtype)


def _matmul(a, b, out_dtype, tm, tn, name):
    m, k = a.shape
    n = b.shape[1]
    tm, tn = _tile(m, tm), _tile(n, tn)
    return pl.pallas_call(
        _matmul_kernel,
        out_shape=jax.ShapeDtypeStruct((m, n), out_dtype),
        grid=(m // tm, n // tn),
        in_specs=[pl.BlockSpec((tm, k), lambda i, j: (i, 0)), pl.BlockSpec((k, tn), lambda i, j: (0, j))],
        out_specs=pl.BlockSpec((tm, tn), lambda i, j: (i, j)),
        compiler_params=_params("parallel", "parallel"),
        name=name,
    )(a, b)


BF16_TERMS = 3


def _split_bf16(x):
    hi = x.astype(BF16)
    r1 = x - hi.astype(F32)
    mid = r1.astype(BF16)
    lo = (r1 - mid.astype(F32)).astype(BF16)
    return hi, mid, lo


def _lower_tri_bf16(t):
    r = lax.broadcasted_iota(jnp.int32, (t, t), 0)
    c = lax.broadcasted_iota(jnp.int32, (t, t), 1)
    return (c <= r).astype(BF16)


def _tri_cumsum(tri, x):
    hi, mid, lo = _split_bf16(x)
    return _dot(tri, hi) + _dot(tri, mid) + _dot(tri, lo)


def _log_sigmoid(x):
    return jnp.minimum(x, 0.0) - jnp.log1p(jnp.exp(-jnp.abs(x)))


FGATE_SCAN_ROWS = 256


def _fgate_bias_kernel(fg_ref, b_ref, sel_ref, kx_ref, carry_ref):
    @pl.when(pl.program_id(0) == 0)
    def _():
        carry_ref[...] = jnp.zeros_like(carry_ref)

    rows = fg_ref.shape[0]
    t = min(FGATE_SCAN_ROWS, rows)
    tri = _lower_tri_bf16(t)
    carry = carry_ref[...]
    for k in range(rows // t):
        ls = _log_sigmoid(fg_ref[k * t:(k + 1) * t, :] + b_ref[...])
        cs = _tri_cumsum(tri, ls) + carry
        carry = cs[t - 1:t, :]
        terms = jnp.concatenate(_split_bf16(cs * (-LOG2E)), axis=1)
        kx_ref[k * t:(k + 1) * t, :] = _dot(terms, sel_ref[...]).astype(BF16)
    carry_ref[...] = carry


def _fgate_bias(fg, bias_row, heads):
    s, w = fg.shape
    tm = _tile(s, ROWS["fgate"])
    row = jnp.arange(BF16_TERMS * LANES)
    col = jnp.arange(heads * HEAD_DIM)
    sel = ((row[:, None] % LANES == col[None, :] // HEAD_DIM)
           & (row[:, None] // LANES == col[None, :] % HEAD_DIM)).astype(BF16)
    return pl.pallas_call(
        _fgate_bias_kernel,
        out_shape=jax.ShapeDtypeStruct((s, heads * HEAD_DIM), BF16),
        grid=(s // tm,),
        in_specs=[pl.BlockSpec((tm, w), lambda i: (i, 0)), pl.BlockSpec((1, w), lambda i: (0, 0)),
                  pl.BlockSpec(sel.shape, lambda i: (0, 0))],
        out_specs=pl.BlockSpec((tm, heads * HEAD_DIM), lambda i: (i, 0)),
        scratch_shapes=[pltpu.VMEM((1, w), F32)],
        compiler_params=_params("arbitrary"),
        name="fgate_bias",
    )(fg, bias_row, sel)


def _attn_kernel(q_ref, k_ref, v_ref, kx_ref, o_ref, sa_ref, sb_ref, m_ref, l_ref, acc_ref, *, tile):
    i = pl.program_id(1)
    lane = lax.broadcasted_iota(jnp.int32, (tile, LANES), 1)
    ones = (lane < BF16_TERMS).astype(BF16)
    q = (q_ref[...].astype(F32) * (HEAD_DIM ** -0.5 * LOG2E)).astype(BF16)
    q_aug = jnp.concatenate([q, ones], axis=1)

    m_ref[...] = jnp.full_like(m_ref, MASK_VALUE)
    l_ref[...] = jnp.zeros_like(l_ref)
    acc_ref[...] = jnp.zeros_like(acc_ref)

    def logits(j):
        k0 = pl.multiple_of(j * tile, tile)
        k_aug = jnp.concatenate([k_ref[pl.ds(k0, tile), :], kx_ref[pl.ds(k0, tile), :]], axis=1)
        return lax.dot_general(q_aug, k_aug, NT_DIMS, preferred_element_type=F32)

    def causal(s):
        row = lax.broadcasted_iota(jnp.int32, s.shape, 0)
        col = lax.broadcasted_iota(jnp.int32, s.shape, 1)
        return jnp.where(col <= row, s, MASK_VALUE)

    def update(s, j):
        k0 = pl.multiple_of(j * tile, tile)
        m_prev = m_ref[...]
        m_new = jnp.maximum(m_prev, jnp.max(s, axis=1, keepdims=True))
        alpha = jnp.exp2(m_prev - m_new)
        p = jnp.exp2(s - m_new)
        l_ref[...] = alpha * l_ref[...] + jnp.sum(p, axis=1, keepdims=True)
        acc_ref[...] = alpha * acc_ref[...] + _dot(p.astype(BF16), v_ref[pl.ds(k0, tile), :])
        m_ref[...] = m_new

    sa_ref[...] = logits(0)

    def pair(jj, carry):
        j = 2 * jj
        sb_ref[...] = logits(j + 1)
        update(sa_ref[...], j)
        sa_ref[...] = logits(j + 2)
        update(sb_ref[...], j + 1)
        return carry

    lax.fori_loop(0, i // 2, pair, 0)

    @pl.when(i % 2 == 0)
    def _():
        update(causal(sa_ref[...]), i)

    @pl.when(i % 2 == 1)
    def _():
        sb_ref[...] = logits(i)
        update(sa_ref[...], i - 1)
        update(causal(sb_ref[...]), i)

    o_ref[...] = (acc_ref[...] / l_ref[...]).astype(o_ref.dtype)


def _attention(zq, kx, heads, q_blk, k_blk, v_blk):
    s = zq.shape[0]
    tile = _tile(s, ROWS["attention"])
    keys = lambda blk: pl.BlockSpec((s, HEAD_DIM), lambda h, i: (0, blk + h))
    return pl.pallas_call(
        functools.partial(_attn_kernel, tile=tile),
        out_shape=jax.ShapeDtypeStruct((s, heads * HEAD_DIM), BF16),
        grid=(heads, s // tile),
        in_specs=[pl.BlockSpec((tile, HEAD_DIM), lambda h, i: (i, q_blk + h)), keys(k_blk), keys(v_blk), keys(0)],
        out_specs=pl.BlockSpec((tile, HEAD_DIM), lambda h, i: (i, h)),
        scratch_shapes=[pltpu.VMEM((tile, tile), F32), pltpu.VMEM((tile, tile), F32),
                        pltpu.VMEM((tile, 1), F32), pltpu.VMEM((tile, 1), F32), pltpu.VMEM((tile, HEAD_DIM), F32)],
        compiler_params=_params("parallel", "arbitrary"),
        name="fox_attention",
    )(zq, zq, zq, kx)


CONV_HALO_ROWS = 32
CONV_ROW_CHUNK = 32


def _conv_kernel(val_ref, gate_ref, pval_ref, pgate_ref, w_ref, b_ref, g_ref, beta_ref, o_ref, u_ref, *, taps):
    tm = val_ref.shape[0]
    halo = pval_ref.shape[0]
    first = pl.program_id(0) == 0
    u_prev = pval_ref[...].astype(F32) * jax.nn.sigmoid(pgate_ref[...].astype(F32))
    u_ref[0, 0:halo, :] = jnp.where(first, 0.0, u_prev)
    u_ref[0, halo:halo + tm, :] = val_ref[...].astype(F32) * jax.nn.sigmoid(gate_ref[...].astype(F32))
    n = halo + tm - SUBLANES
    for sh in range(1, SUBLANES):
        u_ref[sh, 0:n, :] = u_ref[0, sh:sh + n, :]

    rc = min(CONV_ROW_CHUNK, tm)
    for r in range(tm // rc):
        acc = None
        for k in range(taps):
            off = halo - (taps - 1) + k
            lo = r * rc + off - off % SUBLANES
            term = u_ref[off % SUBLANES, lo:lo + rc, :] * w_ref[k:k + 1, :]
            acc = term if acc is None else acc + term
        acc = acc + b_ref[...]
        mu = jnp.mean(acc, axis=-1, keepdims=True)
        xc = acc - mu
        y = xc * lax.rsqrt(jnp.mean(xc * xc, axis=-1, keepdims=True) + EPS) * g_ref[...] + beta_ref[...]
        o_ref[r * rc:(r + 1) * rc, :] = (y * jax.nn.sigmoid(y)).astype(o_ref.dtype)


def _conv_module(z, glu_col, channels, w, b, ln_g, ln_b):
    s = z.shape[0]
    taps = w.shape[0]
    assert taps - 1 <= CONV_HALO_ROWS and glu_col % channels == 0
    tm = _tile(s, ROWS["conv"])
    assert tm % CONV_HALO_ROWS == 0
    vblk = glu_col // channels
    per = tm // CONV_HALO_ROWS
    cur = lambda off: pl.BlockSpec((tm, channels), lambda i: (i, vblk + off))
    prev = lambda off: pl.BlockSpec((CONV_HALO_ROWS, channels), lambda i: (jnp.maximum(i * per - 1, 0), vblk + off))
    vec = pl.BlockSpec((1, channels), lambda i: (0, 0))
    return pl.pallas_call(
        functools.partial(_conv_kernel, taps=taps),
        out_shape=jax.ShapeDtypeStruct((s, channels), BF16),
        grid=(s // tm,),
        in_specs=[cur(0), cur(1), prev(0), prev(1), pl.BlockSpec((taps, channels), lambda i: (0, 0)), vec, vec, vec],
        out_specs=pl.BlockSpec((tm, channels), lambda i: (i, 0)),
        scratch_shapes=[pltpu.VMEM((SUBLANES, CONV_HALO_ROWS + tm, channels), F32)],
        compiler_params=_params("parallel"),
        name="conformer_conv",
    )(z, z, z, z, w, b.reshape(1, -1), ln_g.reshape(1, -1), ln_b.reshape(1, -1))


HGRN_HEADS_PER_STEP = 2


def _hgrn_kernel(q_ref, f_ref, i_ref, g_ref, lb_ref, gain_ref, o_ref, st_ref, *, group):
    @pl.when(pl.program_id(1) == 0)
    def _():
        st_ref[...] = jnp.zeros_like(st_ref)

    rows = q_ref.shape[0]
    c = min(HGRN_CHUNK, rows)
    levels = c.bit_length() - 1
    assert 1 << levels == c
    tri = _lower_tri_bf16(c)
    t_idx = lax.broadcasted_iota(jnp.int32, (c, c), 0)
    s_idx = lax.broadcasted_iota(jnp.int32, (c, c), 1)
    x = t_idx ^ s_idx
    split_level = jnp.full((c, c), -1, jnp.int32)
    for lvl in range(levels):
        split_level = split_level + (x >= (1 << lvl)).astype(jnp.int32)
    split_level = jnp.where(t_idx > s_idx, split_level, -1)
    row_idx = lax.broadcasted_iota(jnp.int32, (c, HEAD_DIM), 0)

    def head_chunk(r0, hh):
        cols = slice(hh * HEAD_DIM, (hh + 1) * HEAD_DIM)
        lb = lb_ref[:, cols]
        zf = f_ref[pl.ds(r0, c), cols].astype(F32)
        forget = lb + (1.0 - lb) * jax.nn.sigmoid(zf)
        log_f = jnp.log(jnp.maximum(forget, MIN_FORGET))
        key = (1.0 - lb) * jax.nn.sigmoid(-zf)
        hq = q_ref[pl.ds(r0, c), cols].astype(F32)
        query = hq * jax.nn.sigmoid(hq) * (HEAD_DIM ** -0.5)
        val = i_ref[pl.ds(r0, c), cols]
        cum = _tri_cumsum(tri, log_f)

        st = st_ref[hh]
        o = lax.dot_general((query * jnp.exp(cum)).astype(BF16), st.astype(BF16), NT_DIMS,
                            preferred_element_type=F32)

        scores = jnp.zeros((c, c), F32)
        seg_end = cum
        for lvl in range(levels):
            h = 1 << lvl
            odd = ((row_idx >> lvl) & 1) == 1
            prev_end = pltpu.roll(seg_end, h, axis=0)
            w = jnp.where(odd, query, key) * jnp.exp(jnp.where(odd, cum - prev_end, seg_end - cum))
            w = w.astype(BF16)
            part = lax.dot_general(w, w, NT_DIMS, preferred_element_type=F32)
            scores = scores + jnp.where(split_level == lvl, part, 0.0)
            seg_end = jnp.where(odd, seg_end, pltpu.roll(seg_end, c - h, axis=0))
        o = o + _dot(scores.astype(BF16), val)
        o = o + jnp.sum(query * key, axis=1, keepdims=True) * val.astype(F32)

        cum_last = cum[c - 1:c, :]
        k_dec = (key * jnp.exp(cum_last - cum)).astype(BF16)
        st_ref[hh] = st * jnp.exp(cum_last) + lax.dot_general(val, k_dec, TN_DIMS, preferred_element_type=F32)

        o = _rms(o, gain_ref[:, cols])
        hg = g_ref[pl.ds(r0, c), cols].astype(F32)
        o_ref[pl.ds(r0, c), cols] = (o * (hg * jax.nn.sigmoid(hg))).astype(o_ref.dtype)

    def step(ci, carry):
        r0 = pl.multiple_of(ci * c, c)
        for hh in range(group):
            head_chunk(r0, hh)
        return carry

    lax.fori_loop(0, rows // c, step, 0)


def _hgrn(z, heads, q_col, f_col, i_col, g_col, lower_bound, gain):
    s = z.shape[0]
    tm = _tile(s, ROWS["hgrn"])
    group =HGRN_HEADS_PER_STEP if heads % HGRN_HEADS_PER_STEP == 0 else 1
    width = group * HEAD_DIM
    assert all(c % width == 0 for c in (q_col, f_col, i_col, g_col))
    col = lambda c0: pl.BlockSpec((tm, width), lambda h, i: (i, c0 // width + h))
    vec = pl.BlockSpec((1, width), lambda h, i: (0, h))
    return pl.pallas_call(
        functools.partial(_hgrn_kernel, group=group),
        out_shape=jax.ShapeDtypeStruct((s, heads * HEAD_DIM), BF16),
        grid=(heads // group, s // tm),
        in_specs=[col(q_col), col(f_col), col(i_col), col(g_col), vec, vec],
        out_specs=pl.BlockSpec((tm, width), lambda h, i: (i, h)),
        scratch_shapes=[pltpu.VMEM((group, HEAD_DIM, HEAD_DIM), F32)],
        compiler_params=_params("parallel", "arbitrary"),
        name="hgrn2",
    )(z, z, z, z, lower_bound.reshape(1, -1), gain.reshape(1, -1))


def _merge_kernel(a_ref, b_ref, c_ref, wa_ref, wb_ref, wc_ref, ga_ref, gb_ref, gc_ref, o_ref):
    gate = lambda ref: jax.nn.sigmoid(ref[...].astype(F32))
    merged = gate(ga_ref) * _dot(a_ref[...], wa_ref[...])
    merged = merged + gate(gb_ref) * _dot(b_ref[...], wb_ref[...])
    merged = merged + gate(gc_ref) * _dot(c_ref[...], wc_ref[...])
    o_ref[...] = merged.astype(o_ref.dtype)


def _merge(att, conv, hgrn, wa, wb, wc, z, gate_col):
    s = att.shape[0]
    d = wa.shape[1]
    tm, tn = _tile(s, ROWS["merge"]), _tile(d, 512)
    assert gate_col % tn == 0 and d % tn == 0
    act = lambda a: pl.BlockSpec((tm, a.shape[1]), lambda i, j: (i, 0))
    wgt = lambda w: pl.BlockSpec((w.shape[0], tn), lambda i, j: (0, j))
    gate = lambda n: pl.BlockSpec((tm, tn), lambda i, j: (i, (gate_col + n * d) // tn + j))
    return pl.pallas_call(
        _merge_kernel,
        out_shape=jax.ShapeDtypeStruct((s, d), BF16),
        grid=(s // tm, d // tn),
        in_specs=[act(att), act(conv), act(hgrn), wgt(wa), wgt(wb), wgt(wc), gate(0), gate(1), gate(2)],
        out_specs=pl.BlockSpec((tm, tn), lambda i, j: (i, j)),
        compiler_params=_params("parallel", "parallel"),
        name="branch_merge",
    )(att, conv, hgrn, wa, wb, wc, z, z, z)


FFN_ROW_SUB = 256


def _ffn_up_kernel(h_ref, wg_ref, wu_ref, dwg_ref, dwu_ref, o_ref, carry_ref, ext_ref, *, taps, sub):
    i = pl.program_id(0)
    j = pl.program_id(1)
    tm = h_ref.shape[0]
    halo = SUBLANES

    @pl.when(i == 0)
    def _():
        carry_ref[j] = jnp.zeros(carry_ref.shape[1:], F32)

    prev = [carry_ref[j, 0], carry_ref[j, 1]]
    for r in range(tm // sub):
        hr = h_ref[r * sub:(r + 1) * sub, :]
        conv = []
        for n, (w_ref, dw_ref) in enumerate(((wg_ref, dwg_ref), (wu_ref, dwu_ref))):
            u = _dot(hr, w_ref[...])
            ext_ref[r, n, 0:halo, :] = prev[n]
            ext_ref[r, n, halo:halo + sub, :] = u
            prev[n] = u[sub - halo:sub, :]
            base = halo - (taps - 1)
            acc = ext_ref[r, n, base:base + sub, :] * dw_ref[0:1, :]
            for k in range(1, taps):
                acc = acc + ext_ref[r, n, base + k:base + k + sub, :] * dw_ref[k:k + 1, :]
            conv.append(acc)
        gate, up = conv
        o_ref[r * sub:(r + 1) * sub, :] = (gate * jax.nn.sigmoid(gate) * up).astype(o_ref.dtype)
    carry_ref[j, 0] = prev[0]
    carry_ref[j, 1] = prev[1]


def _ffn_up(h, w_up, dw, d_ff):
    s, d = h.shape
    taps = dw.shape[0]
    assert taps - 1 <= SUBLANES
    tm, tn = _tile(s, ROWS["ffn_up"]), _tile(d_ff, 256)
    sub = _tile(tm, FFN_ROW_SUB)
    nj = d_ff // tn
    return pl.pallas_call(
        functools.partial(_ffn_up_kernel, taps=taps, sub=sub),
        out_shape=jax.ShapeDtypeStruct((s, d_ff), BF16),
        grid=(s // tm, nj),
        in_specs=[
            pl.BlockSpec((tm, d), lambda i, j: (i, 0)),
            pl.BlockSpec((d, tn), lambda i, j: (0, j)),
            pl.BlockSpec((d, tn), lambda i, j: (0, nj + j)),
            pl.BlockSpec((taps, tn), lambda i, j: (0, j)),
            pl.BlockSpec((taps, tn), lambda i, j: (0, nj + j)),
        ],
        out_specs=pl.BlockSpec((tm, tn), lambda i, j: (i, j)),
        scratch_shapes=[pltpu.VMEM((nj, 2, SUBLANES, tn), F32), pltpu.VMEM((tm // sub, 2, SUBLANES + sub, tn), F32)],
        compiler_params=_params("arbitrary", "arbitrary"),
        name="ffn_up_conv_glu",
    )(h, w_up, w_up, dw, dw)


def _layer(x, h, p, g_next):
    heads = p["b_fgate"].shape[0]
    attn_w = heads * HEAD_DIM
    conv_c = p["conv_b"].shape[0]
    hgrn_w = p["hgrn_norm_g"].shape[0]
    d_ff = p["w_ffn_down"].shape[0]

    w_in = p["w_in"]
    fg0 = 3 * attn_w
    zq = _matmul(h, w_in[:, :fg0].astype(BF16), BF16, 1024, 1024, "in_proj_qkv")
    z = _matmul(h, w_in[:, fg0 + heads:].astype(BF16), BF16, 1024, 1024, "in_proj_rest")
    w_fg = jnp.pad(w_in[:, fg0:fg0 + heads], ((0, 0), (0, LANES - heads))).astype(BF16)
    fg = _matmul(h, w_fg, F32, 1024, LANES, "in_proj_fgate")
    hgrn_col = 2 * conv_c
    gate_col = hgrn_col + 4 * hgrn_w

    bias_row = jnp.pad(p["b_fgate"].astype(F32), (0, LANES - heads)).reshape(1, LANES)
    kx = _fgate_bias(fg, bias_row, heads)
    att = _attention(zq, kx, heads, 0, heads, 2 * heads)

    conv = _conv_module(z, 0, conv_c, p["conv_dw"], p["conv_b"], p["conv_ln_g"], p["conv_ln_b"])

    hg = _hgrn(z, hgrn_w // HEAD_DIM, hgrn_col, hgrn_col + hgrn_w, hgrn_col + 2 * hgrn_w, hgrn_col + 3 * hgrn_w,
               p["lower_bound"], p["hgrn_norm_g"])

    merged = _merge(att, conv, hg, p["w_attn_out"].astype(BF16), p["w_conv_out"].astype(BF16),
                    p["w_hgrn_out"].astype(BF16), z, gate_col)
    y = _matmul(merged, p["w_o"].astype(BF16), F32, 1024, 1024, "out_proj")
    x, h = _residual_norm(x, y, p["norm_gains"][1], p["norm_gains"][2])

    act = _ffn_up(h, p["w_ffn_up"].astype(BF16), p["ffn_dw"], d_ff)
    y = _matmul(act, p["w_ffn_down"].astype(BF16), F32, 512, 512, "ffn_down")
    return _residual_norm(x, y, p["norm_gains"][3], g_next)


def kernel(x, norm_gains, w_in, b_fgate, conv_dw, conv_b, conv_ln_g, conv_ln_b, hgrn_lb_logits, hgrn_norm_g,
           w_attn_out, w_conv_out, w_hgrn_out, w_o, w_ffn_up, ffn_dw, w_ffn_down):
    b, s, d = x.shape
    assert b == 1, "the sequence scans assume a single sequence"
    depth = w_in.shape[0]
    p_lb = jax.nn.softmax(hgrn_lb_logits.astype(F32), axis=0)
    lower_bounds = jnp.cumsum(p_lb, axis=0) - p_lb[0]

    xs = x.reshape(s, d)
    h = _rmsnorm_bf16(xs, norm_gains[0, 0])
    for l in range(depth):
        p = dict(norm_gains=norm_gains[l], w_in=w_in[l], b_fgate=b_fgate[l], conv_dw=conv_dw[l], conv_b=conv_b[l],
                 conv_ln_g=conv_ln_g[l], conv_ln_b=conv_ln_b[l], lower_bound=lower_bounds[l],
                 hgrn_norm_g=hgrn_norm_g[l], w_attn_out=w_attn_out[l], w_conv_out=w_conv_out[l],
                 w_hgrn_out=w_hgrn_out[l], w_o=w_o[l], w_ffn_up=w_ffn_up[l], ffn_dw=ffn_dw[l],
                 w_ffn_down=w_ffn_down[l])
        g_next = norm_gains[l + 1, 0] if l + 1 < depth else None
        xs, h = _layer(xs, h, p, g_next)
    return xs.reshape(b, s, d)
```

```python
import functools

import jax
import jax.numpy as jnp
from jax import lax
from jax.experimental import pallas as pl
from jax.experimental.pallas import tpu as pltpu

F32 = jnp.float32
BF16 = jnp.bfloat16

HEAD_DIM = 128
LANES = 128
SUBLANES = 8
MXU_DIM = 256
ATTN_SOFTMAX_ROWS = 128
EPS = 1e-6
MASK_VALUE = -1e30
MIN_FORGET = 1e-30
LOG2E = 1.4426950408889634
HGRN_CHUNK = 128
V7X_VMEM_LIMIT_BYTES = 56 * 1024 * 1024

NT_DIMS = (((1,), (1,)), ((), ()))
TN_DIMS = (((0,), (0,)), ((), ()))

ROWS = dict(rmsnorm=512, residual=256, fgate=2048, attention=1024, conv=128, hgrn=1024, merge=1024, ffn_up=1024)


def _params(*semantics):
    return pltpu.CompilerParams(dimension_semantics=semantics, vmem_limit_bytes=V7X_VMEM_LIMIT_BYTES)


def _tile(n, preferred):
    t = preferred
    while t >= 8:
        if n % t == 0:
            return t
        t //= 2
    return n


_dot = functools.partial(jnp.dot, preferred_element_type=F32)


def _rms(x, g):
    return x * lax.rsqrt(jnp.mean(x * x, axis=-1, keepdims=True) + EPS) * g


def _rmsnorm_kernel(x_ref, g_ref, h_ref):
    h_ref[...] = _rms(x_ref[...], g_ref[...]).astype(h_ref.dtype)


def _rmsnorm_bf16(x, g):
    s, d = x.shape
    tm = _tile(s, ROWS["rmsnorm"])
    return pl.pallas_call(
        _rmsnorm_kernel,
        out_shape=jax.ShapeDtypeStruct((s, d), BF16),
        grid=(s // tm,),
        in_specs=[pl.BlockSpec((tm, d), lambda i: (i, 0)), pl.BlockSpec((1, d), lambda i: (0, 0))],
        out_specs=pl.BlockSpec((tm, d), lambda i: (i, 0)),
        compiler_params=_params("parallel"),
        name="rmsnorm",
    )(x, g.reshape(1, d))


def _residual_norm_kernel(x_ref, y_ref, g_post_ref, g_next_ref, x_out_ref, h_ref):
    x_new = x_ref[...] + _rms(y_ref[...], g_post_ref[...])
    x_out_ref[...] = x_new
    h_ref[...] = _rms(x_new, g_next_ref[...]).astype(h_ref.dtype)


def _residual_kernel(x_ref, y_ref, g_post_ref, x_out_ref):
    x_out_ref[...] = x_ref[...] + _rms(y_ref[...], g_post_ref[...])


def _residual_norm(x, y, g_post, g_next):
    s, d = x.shape
    tm = _tile(s, ROWS["residual"])
    row = pl.BlockSpec((tm, d), lambda i: (i, 0))
    vec = pl.BlockSpec((1, d), lambda i: (0, 0))
    x_shape = jax.ShapeDtypeStruct((s, d), F32)
    if g_next is None:
        return pl.pallas_call(
            _residual_kernel, out_shape=x_shape, grid=(s // tm,), in_specs=[row, row, vec], out_specs=row,
            compiler_params=_params("parallel"), name="residual",
        )(x, y, g_post.reshape(1, d)), None
    return pl.pallas_call(
        _residual_norm_kernel,
        out_shape=(x_shape, jax.ShapeDtypeStruct((s, d), BF16)),
        grid=(s // tm,),
        in_specs=[row, row, vec, vec],
        out_specs=(row, row),
        compiler_params=_params("parallel"),
        name="residual_norm",
    )(x, y, g_post.reshape(1, d), g_next.reshape(1, d))


def _matmul_kernel(a_ref, b_ref, o_ref):
    o_ref[...] = _dot(a_ref[...], b_ref[...]).astype(o_ref.dtype)


def _matmul(a, b, out_dtype, tm, tn, name):
    m, k = a.shape
    n = b.shape[1]
    tm, tn = _tile(m, tm), _tile(n, tn)
    return pl.pallas_call(
        _matmul_kernel,
        out_shape=jax.ShapeDtypeStruct((m, n), out_dtype),
        grid=(m // tm, n // tn),
        in_specs=[pl.BlockSpec((tm, k), lambda i, j: (i, 0)), pl.BlockSpec((k, tn), lambda i, j: (0, j))],
        out_specs=pl.BlockSpec((tm, tn), lambda i, j: (i, j)),
        compiler_params=_params("parallel", "parallel"),
        name=name,
    )(a, b)


CAST_BLOCK_BYTES = 4 * 1024 * 1024


def _cast_kernel(w_ref, o_ref):
    o_ref[...] = w_ref[...].astype(o_ref.dtype)


def _cast_bf16(w, layer, col0=0, ncols=None):
    _, k, n = w.shape
    ncols = n - col0 if ncols is None else ncols
    tn = _tile(ncols, 4096)
    while col0 % tn:
        tn //= 2
    assert tn % LANES == 0 and ncols % tn == 0
    tk = _tile(k, max(CAST_BLOCK_BYTES // (4 * tn), 16))
    return pl.pallas_call(
        _cast_kernel,
        out_shape=jax.ShapeDtypeStruct((k, ncols), BF16),
        grid=(k // tk, ncols // tn),
        in_specs=[pl.BlockSpec((None, tk, tn), lambda a, b: (layer, a, col0 // tn + b))],
        out_specs=pl.BlockSpec((tk, tn), lambda a, b: (a, b)),
        compiler_params=_params("parallel", "parallel"),
        name="cast_bf16",
    )(w)


BF16_TERMS = 3


def _split_bf16(x):
    hi = x.astype(BF16)
    r1 = x - hi.astype(F32)
    mid = r1.astype(BF16)
    lo = (r1 - mid.astype(F32)).astype(BF16)
    return hi, mid, lo


def _lower_tri_bf16(t):
    r = lax.broadcasted_iota(jnp.int32, (t, t), 0)
    c = lax.broadcasted_iota(jnp.int32, (t, t), 1)
    return (c <= r).astype(BF16)


def _tri_cumsum(tri, x):
    hi, mid, lo = _split_bf16(x)
    return _dot(tri, hi) + _dot(tri, mid) + _dot(tri, lo)


def _log_sigmoid(x):
    return jnp.minimum(x, 0.0) - jnp.log1p(jnp.exp(-jnp.abs(x)))


FGATE_SCAN_ROWS = 256


def _fgate_bias_kernel(fg_ref, b_ref, sel_ref, kx_ref, carry_ref):
    @pl.when(pl.program_id(0) == 0)
    def _():
        carry_ref[...] = jnp.zeros_like(carry_ref)

    rows = fg_ref.shape[0]
    t = min(FGATE_SCAN_ROWS, rows)
    tri = _lower_tri_bf16(t)
    carry = carry_ref[...]
    for k in range(rows // t):
        ls = _log_sigmoid(fg_ref[k * t:(k + 1) * t, :] + b_ref[...])
        cs = _tri_cumsum(tri, ls) + carry
        carry = cs[t - 1:t, :]
        terms = jnp.concatenate(_split_bf16(cs * (-LOG2E)), axis=1)
        kx_ref[k * t:(k + 1) * t, :] = _dot(terms, sel_ref[...]).astype(BF16)
    carry_ref[...] = carry


def _fgate_bias(fg, bias_row, heads):
    s, w = fg.shape
    tm = _tile(s, ROWS["fgate"])
    row = jnp.arange(BF16_TERMS * LANES)
    col = jnp.arange(heads * HEAD_DIM)
    sel = ((row[:, None] % LANES == col[None, :] // HEAD_DIM)
           & (row[:, None] // LANES == col[None, :] % HEAD_DIM)).astype(BF16)
    return pl.pallas_call(
        _fgate_bias_kernel,
        out_shape=jax.ShapeDtypeStruct((s, heads * HEAD_DIM), BF16),
        grid=(s // tm,),
        in_specs=[pl.BlockSpec((tm, w), lambda i: (i, 0)), pl.BlockSpec((1, w), lambda i: (0, 0)),
                  pl.BlockSpec(sel.shape, lambda i: (0, 0))],
        out_specs=pl.BlockSpec((tm, heads * HEAD_DIM), lambda i: (i, 0)),
        scratch_shapes=[pltpu.VMEM((1, w), F32)],
        compiler_params=_params("arbitrary"),
        name="fgate_bias",
    )(fg, bias_row, sel)


def _attn_kernel(q_ref, k_ref, v_ref, kx_ref, o_ref, sa_ref, sb_ref, pa_ref, pb_ref, aa_ref, ab_ref, m_ref, acc_ref,
                 ka_ref, *, tile):
    i = pl.program_id(1)
    lane = lax.broadcasted_iota(jnp.int32, (tile, LANES), 1)
    bias_ones = (lane < BF16_TERMS).astype(BF16)
    sum_ones = (lane == 0).astype(BF16)
    q = (q_ref[...].astype(F32) * (HEAD_DIM ** -0.5 * LOG2E)).astype(BF16)
    q_aug = jnp.concatenate([q, bias_ones], axis=1)

    m_ref[...] = jnp.full_like(m_ref, MASK_VALUE)
    acc_ref[...] = jnp.zeros_like(acc_ref)

    def softmax(s_ref, p_ref, a_ref, diagonal=False):
        zeros = []
        for r in range(tile // ATTN_SOFTMAX_ROWS):
            rows = slice(r * ATTN_SOFTMAX_ROWS, (r + 1) * ATTN_SOFTMAX_ROWS)
            s = s_ref[rows, :]
            if diagonal:
                row = lax.broadcasted_iota(jnp.int32, s.shape, 0) + r * ATTN_SOFTMAX_ROWS
                col = lax.broadcasted_iota(jnp.int32, s.shape, 1)
                s = jnp.where(col <= row, s, MASK_VALUE)
            m_prev = m_ref[rows, :]
            m_new = jnp.maximum(m_prev, jnp.max(s, axis=1, keepdims=True))
            a_ref[rows, :] = jnp.exp2(m_prev - m_new)
            p = jnp.exp2(s - m_new).astype(BF16)
            p_ref[rows, :] = p
            m_ref[rows, :] = m_new
            bits = pltpu.bitcast(p[0:2 * SUBLANES, 0:LANES], jnp.uint32)
            zeros.append(pltpu.bitcast((bits >> 16) >> 16, BF16))
        return zeros

    def logits(j, s_ref, gates):
        k0 = pl.multiple_of(j * tile, tile)
        ka_ref[:, 0:HEAD_DIM] = k_ref[pl.ds(k0, tile), :]
        ka_ref[:, HEAD_DIM:2 * HEAD_DIM] = kx_ref[pl.ds(k0, tile), :]
        for n, zero in gates:
            rows = slice(n * MXU_DIM, n * MXU_DIM + 2 * SUBLANES)
            ka_ref[rows, 0:LANES] = ka_ref[rows, 0:LANES] + zero
        s_ref[...] = lax.dot_general(q_aug, ka_ref[...], NT_DIMS, preferred_element_type=F32)

    def accumulate(j, p_ref, a_ref, gates):
        k0 = pl.multiple_of(j * tile, tile)
        for n, zero in gates:
            rows = slice(n * MXU_DIM, n * MXU_DIM + 2 * SUBLANES)
            cols = slice(n * MXU_DIM, n * MXU_DIM + LANES)
            p_ref[rows, cols] = p_ref[rows, cols] + zero
        v_aug = jnp.concatenate([v_ref[pl.ds(k0, tile), :], sum_ones], axis=1)
        acc_ref[...] = a_ref[...] * acc_ref[...] + _dot(p_ref[...], v_aug)

    def stage(soft=None, qk=None, pv=None):
        zeros = softmax(*soft) if soft is not None else []
        blocks = tile // MXU_DIM
        slots = ([("qk", n) for n in range(1, blocks)] if qk is not None else []) + \
                ([("pv", n) for n in range(0 if qk is not None else 1, blocks)] if pv is not None else [])
        gates = dict(qk=[], pv=[])
        for (which, n), zero in zip(slots, zeros):
            gates[which].append((n, zero))
        if qk is not None:
            logits(*qk, gates["qk"])
        if pv is not None:
            accumulate(*pv, gates["pv"])

    stage(qk=(0, sa_ref))

    @pl.when(i == 0)
    def _():
        stage(soft=(sa_ref, pa_ref, aa_ref, True))
        stage(pv=(0, pa_ref, aa_ref))

    @pl.when(i > 0)
    def _():
        stage(soft=(sa_ref, pa_ref, aa_ref), qk=(1, sb_ref))

    def pair(jj, carry):
        j = 2 * jj + 1
        stage(soft=(sb_ref, pb_ref, ab_ref), qk=(j + 1, sa_ref), pv=(j - 1, pa_ref, aa_ref))
        stage(soft=(sa_ref, pa_ref, aa_ref), qk=(j + 2, sb_ref), pv=(j, pb_ref, ab_ref))
        return carry

    lax.fori_loop(0, jnp.maximum(i - 1, 0) // 2, pair, 0)

    @pl.when(i % 2 == 1)
    def _():
        stage(soft=(sb_ref, pb_ref, ab_ref, True), pv=(i - 1, pa_ref, aa_ref))
        stage(pv=(i, pb_ref, ab_ref))

    @pl.when((i > 0) & (i % 2 == 0))
    def _():
        stage(soft=(sb_ref, pb_ref, ab_ref), qk=(i, sa_ref), pv=(i - 2, pa_ref, aa_ref))
        stage(soft=(sa_ref, pa_ref, aa_ref, True), pv=(i - 1, pb_ref, ab_ref))
        stage(pv=(i, pa_ref, aa_ref))

    o_ref[...] = (acc_ref[:, 0:HEAD_DIM] / acc_ref[:, HEAD_DIM:HEAD_DIM + 1]).astype(o_ref.dtype)


def _attention(zq, kx, heads, q_blk, k_blk, v_blk):
    s = zq.shape[0]
    tile = _tile(s, ROWS["attention"])
    keys = lambda blk: pl.BlockSpec((s, HEAD_DIM), lambda h, i: (0, blk + h))
    return pl.pallas_call(
        functools.partial(_attn_kernel, tile=tile),
        out_shape=jax.ShapeDtypeStruct((s, heads * HEAD_DIM), BF16),
        grid=(heads, s // tile),
        in_specs=[pl.BlockSpec((tile, HEAD_DIM), lambda h, i: (i, q_blk + h)), keys(k_blk), keys(v_blk), keys(0)],
        out_specs=pl.BlockSpec((tile, HEAD_DIM), lambda h, i: (i, h)),
        scratch_shapes=[pltpu.VMEM((tile, tile), F32), pltpu.VMEM((tile, tile), F32),
                        pltpu.VMEM((tile, tile), BF16), pltpu.VMEM((tile, tile), BF16),
                        pltpu.VMEM((tile, 1), F32), pltpu.VMEM((tile, 1), F32),
                        pltpu.VMEM((tile, 1), F32), pltpu.VMEM((tile, 2 * HEAD_DIM), F32),
                        pltpu.VMEM((tile, 2 * HEAD_DIM), BF16)],
        compiler_params=_params("parallel", "arbitrary"),
        name="fox_attention",
    )(zq, zq, zq, kx)


CONV_HALO_ROWS = 32
CONV_ROW_CHUNK = 32


def _conv_kernel(val_ref, gate_ref, pval_ref, pgate_ref, w_ref, b_ref, g_ref, beta_ref, o_ref, u_ref, *, taps):
    tm = val_ref.shape[0]
    halo = pval_ref.shape[0]
    first = pl.program_id(0) == 0
    u_prev = pval_ref[...].astype(F32) * jax.nn.sigmoid(pgate_ref[...].astype(F32))
    u_ref[0, 0:halo, :] = jnp.where(first, 0.0, u_prev)
    u_ref[0, halo:halo + tm, :] = val_ref[...].astype(F32) * jax.nn.sigmoid(gate_ref[...].astype(F32))
    n = halo + tm - SUBLANES
    for sh in range(1, SUBLANES):
        u_ref[sh, 0:n, :] = u_ref[0, sh:sh + n, :]

    rc = min(CONV_ROW_CHUNK, tm)
    for r in range(tm // rc):
        acc = None
        for k in range(taps):
            off = halo - (taps - 1) + k
            lo = r * rc + off - off % SUBLANES
            term = u_ref[off % SUBLANES, lo:lo + rc, :] * w_ref[k:k + 1, :]
            acc = term if acc is None else acc + term
        acc = acc + b_ref[...]
        mu = jnp.mean(acc, axis=-1, keepdims=True)
        xc = acc - mu
        y = xc * lax.rsqrt(jnp.mean(xc * xc, axis=-1, keepdims=True) + EPS) * g_ref[...] + beta_ref[...]
        o_ref[r * rc:(r + 1) * rc, :] = (y * jax.nn.sigmoid(y)).astype(o_ref.dtype)


def _conv_module(z, glu_col, channels, w, b, ln_g, ln_b):
    s = z.shape[0]
    taps = w.shape[0]
    assert taps - 1 <= CONV_HALO_ROWS and glu_col % channels == 0
    tm = _tile(s, ROWS["conv"])
    assert tm % CONV_HALO_ROWS == 0
    vblk = glu_col // channels
    per = tm // CONV_HALO_ROWS
    cur = lambda off: pl.BlockSpec((tm, channels), lambda i: (i, vblk + off))
    prev = lambda off: pl.BlockSpec((CONV_HALO_ROWS, channels), lambda i: (jnp.maximum(i * per - 1, 0), vblk + off))
    vec = pl.BlockSpec((1, channels), lambda i: (0, 0))
    return pl.pallas_call(
        functools.partial(_conv_kernel, taps=taps),
        out_shape=jax.ShapeDtypeStruct((s, channels), BF16),
        grid=(s // tm,),
        in_specs=[cur(0), cur(1), prev(0), prev(1), pl.BlockSpec((taps, channels), lambda i: (0, 0)), vec, vec, vec],
        out_specs=pl.BlockSpec((tm, channels), lambda i: (i, 0)),
        scratch_shapes=[pltpu.VMEM((SUBLANES, CONV_HALO_ROWS + tm, channels), F32)],
        compiler_params=_params("parallel"),
        name="conformer_conv",
    )(z, z, z, z, w, b.reshape(1, -1), ln_g.reshape(1, -1), ln_b.reshape(1, -1))


HGRN_HEADS_PER_STEP = 2


def _hgrn_kernel(q_ref, f_ref, i_ref, g_ref, lb_ref, gain_ref, o_ref, st_ref, *, group):
    @pl.when(pl.program_id(1) == 0)
    def _():
        st_ref[...] = jnp.zeros_like(st_ref)

    rows = q_ref.shape[0]
    c = min(HGRN_CHUNK, rows)
    levels = c.bit_length() - 1
    assert 1 << levels == c
    tri = _lower_tri_bf16(c)
    t_idx = lax.broadcasted_iota(jnp.int32, (c, c), 0)
    s_idx = lax.broadcasted_iota(jnp.int32, (c, c), 1)
    x = t_idx ^ s_idx
    split_level = jnp.full((c, c), -1, jnp.int32)
    for lvl in range(levels):
        split_level = split_level + (x >= (1 << lvl)).astype(jnp.int32)
    split_level = jnp.where(t_idx > s_idx, split_level, -1)
    row_idx = lax.broadcasted_iota(jnp.int32, (c, HEAD_DIM), 0)

    def head_chunk(r0, hh):
        cols = slice(hh * HEAD_DIM, (hh + 1) * HEAD_DIM)
        lb = lb_ref[:, cols]
        zf = f_ref[pl.ds(r0, c), cols].astype(F32)
        forget = lb + (1.0 - lb) * jax.nn.sigmoid(zf)
        log_f = jnp.log(jnp.maximum(forget, MIN_FORGET))
        key = (1.0 - lb) * jax.nn.sigmoid(-zf)
        hq = q_ref[pl.ds(r0, c), cols].astype(F32)
        query = hq * jax.nn.sigmoid(hq) * (HEAD_DIM ** -0.5)
        val = i_ref[pl.ds(r0, c), cols]
        cum = _tri_cumsum(tri, log_f)

        st = st_ref[hh]
        o = lax.dot_general((query * jnp.exp(cum)).astype(BF16), st.astype(BF16), NT_DIMS,
                            preferred_element_type=F32)

        scores = jnp.zeros((c, c), F32)
        seg_end = cum
        for lvl in range(levels):
            h = 1 << lvl
            odd = ((row_idx >> lvl) & 1) == 1
            prev_end = pltpu.roll(seg_end, h, axis=0)
            w = jnp.where(odd, query, key) * jnp.exp(jnp.where(odd, cum - prev_end, seg_end - cum))
            w = w.astype(BF16)
            part = lax.dot_general(w, w, NT_DIMS, preferred_element_type=F32)
            scores = scores + jnp.where(split_level == lvl, part, 0.0)
            seg_end = jnp.where(odd, seg_end, pltpu.roll(seg_end, c - h, axis=0))
        o = o + _dot(scores.astype(BF16), val)
        o = o + jnp.sum(query * key, axis=1, keepdims=True) * val.astype(F32)

        cum_last = cum[c - 1:c, :]
        k_dec = (key * jnp.exp(cum_last - cum)).astype(BF16)
        st_ref[hh] = st * jnp.exp(cum_last) + lax.dot_general(val, k_dec, TN_DIMS, preferred_element_type=F32)

        o = _rms(o, gain_ref[:, cols])
        hg = g_ref[pl.ds(r0, c), cols].astype(F32)
        o_ref[pl.ds(r0, c), cols] = (o * (hg * jax.nn.sigmoid(hg))).astype(o_ref.dtype)

    def step(ci, carry):
        r0 = pl.multiple_of(ci * c, c)
        for hh in range(group):
            head_chunk(r0, hh)
        return carry

    lax.fori_loop(0, rows // c, step, 0)


def _hgrn(z, heads, q_col, f_col, i_col, g_col, lower_bound, gain):
    s = z.shape[0]
    tm = _tile(s, ROWS["hgrn"])
    group =HGRN_HEADS_PER_STEP if heads % HGRN_HEADS_PER_STEP == 0 else 1
    width = group * HEAD_DIM
    assert all(c % width == 0 for c in (q_col, f_col, i_col, g_col))
    col = lambda c0: pl.BlockSpec((tm, width), lambda h, i: (i, c0 // width + h))
    vec = pl.BlockSpec((1, width), lambda h, i: (0, h))
    return pl.pallas_call(
        functools.partial(_hgrn_kernel, group=group),
        out_shape=jax.ShapeDtypeStruct((s, heads * HEAD_DIM), BF16),
        grid=(heads // group, s // tm),
        in_specs=[col(q_col), col(f_col), col(i_col), col(g_col), vec, vec],
        out_specs=pl.BlockSpec((tm, width), lambda h, i: (i, h)),
        scratch_shapes=[pltpu.VMEM((group, HEAD_DIM, HEAD_DIM), F32)],
        compiler_params=_params("parallel", "arbitrary"),
        name="hgrn2",
    )(z, z, z, z, lower_bound.reshape(1, -1), gain.reshape(1, -1))


def _merge_kernel(a_ref, b_ref, c_ref, wa_ref, wb_ref, wc_ref, ga_ref, gb_ref, gc_ref, o_ref):
    gate = lambda ref: jax.nn.sigmoid(ref[...].astype(F32))
    merged = gate(ga_ref) * _dot(a_ref[...], wa_ref[...])
    merged = merged + gate(gb_ref) * _dot(b_ref[...], wb_ref[...])
    merged = merged + gate(gc_ref) * _dot(c_ref[...], wc_ref[...])
    o_ref[...] = merged.astype(o_ref.dtype)


def _merge(att, conv, hgrn, wa, wb, wc, z, gate_col):
    s = att.shape[0]
    d = wa.shape[1]
    tm, tn = _tile(s, ROWS["merge"]), _tile(d, 512)
    assert gate_col % tn == 0 and d % tn == 0
    act = lambda a: pl.BlockSpec((tm, a.shape[1]), lambda i, j: (i, 0))
    wgt = lambda w: pl.BlockSpec((w.shape[0], tn), lambda i, j: (0, j))
    gate = lambda n: pl.BlockSpec((tm, tn), lambda i, j: (i, (gate_col + n * d) // tn + j))
    return pl.pallas_call(
        _merge_kernel,
        out_shape=jax.ShapeDtypeStruct((s, d), BF16),
        grid=(s // tm, d // tn),
        in_specs=[act(att), act(conv), act(hgrn), wgt(wa), wgt(wb), wgt(wc), gate(0), gate(1), gate(2)],
        out_specs=pl.BlockSpec((tm, tn), lambda i, j: (i, j)),
        compiler_params=_params("parallel", "parallel"),
        name="branch_merge",
    )(att, conv, hgrn, wa, wb, wc, z, z, z)


FFN_ROW_CHUNK = 64


def _ffn_up_kernel(h_ref, wg_ref, wu_ref, dwg_ref, dwu_ref, o_ref, carry_ref, ext_ref, lhs_ref, *, taps, nj, rc):
    i = pl.program_id(0)
    j = pl.program_id(1)
    tm = h_ref.shape[0]
    halo = SUBLANES
    jp = jnp.maximum(j - 1, 0)

    @pl.when((i == 0) & (j < nj))
    def _():
        carry_ref[j] = jnp.zeros(carry_ref.shape[1:], F32)

    @pl.when(j == 0)
    def _():
        lhs_ref[...] = h_ref[...]

    def finish_halo(slot):
        for n in range(2):
            ext_ref[slot, n, 0:halo, :] = carry_ref[jp, n]
            carry_ref[jp, n] = ext_ref[slot, n, tm:tm + halo, :]

    def finish(slot, r):
        lo = halo - (taps - 1) + r * rc
        conv = []
        for n, dw_ref in enumerate((dwg_ref, dwu_ref)):
            acc = ext_ref[slot, n, lo:lo + rc, :] * dw_ref[0:1, :]
            for k in range(1, taps):
                acc = acc + ext_ref[slot, n, lo + k:lo + k + rc, :] * dw_ref[k:k + 1, :]
            conv.append(acc)
        gate, up = conv
        out = (gate * jax.nn.sigmoid(gate) * up).astype(o_ref.dtype)
        o_ref[r * rc:(r + 1) * rc, :] = out
        return out

    def pace(r, out):
        nchunks = tm // rc
        bits = pltpu.bitcast(out[0:2 * SUBLANES, 0:LANES], jnp.uint32)
        zero = pltpu.bitcast((bits >> 16) >> 16, BF16)
        row0 = (tm // nchunks) * r
        col0 = ((lhs_ref.shape[1] // LANES) * r // nchunks) * LANES
        rows, cols = slice(row0, row0 + 2 * SUBLANES), slice(col0, col0 + LANES)
        lhs_ref[rows, cols] = lhs_ref[rows, cols] + zero

    def step(project_slot, finish_slot):
        if finish_slot is not None:
            finish_halo(finish_slot)
            for r in range(tm // rc):
                out = finish(finish_slot, r)
                if project_slot is not None:
                    pace(r, out)
        if project_slot is not None:
            ext_ref[project_slot, 0, halo:halo + tm, :] = _dot(lhs_ref[...], wg_ref[...])
            ext_ref[project_slot, 1, halo:halo + tm, :] = _dot(lhs_ref[...], wu_ref[...])

    interior = (j > 0) & (j < nj)
    pl.when(j == 0)(lambda: step(0, None))
    pl.when(interior & (j % 2 == 0))(lambda: step(0, 1))
    pl.when(interior & (j % 2 == 1))(lambda: step(1, 0))
    pl.when(j == nj)(lambda: step(None, (nj - 1) % 2))


def _ffn_up(h, w_up, dw, d_ff):
    s, d = h.shape
    taps = dw.shape[0]
    assert taps - 1 <= SUBLANES
    tm, tn = _tile(s, ROWS["ffn_up"]), _tile(d_ff, 256)
    nj = d_ff // tn
    cur = lambda j: jnp.minimum(j, nj - 1)
    fin = lambda j: jnp.maximum(j - 1, 0)
    return pl.pallas_call(
        functools.partial(_ffn_up_kernel, taps=taps, nj=nj, rc=_tile(tm, FFN_ROW_CHUNK)),
        out_shape=jax.ShapeDtypeStruct((s, d_ff), BF16),
        grid=(s // tm, nj + 1),
        in_specs=[
            pl.BlockSpec((tm, d), lambda i, j: (i, 0)),
            pl.BlockSpec((d, tn), lambda i, j: (0, cur(j))),
            pl.BlockSpec((d, tn), lambda i, j: (0, nj + cur(j))),
            pl.BlockSpec((taps, tn), lambda i, j: (0, fin(j))),
            pl.BlockSpec((taps, tn), lambda i, j: (0, nj + fin(j))),
        ],
        out_specs=pl.BlockSpec((tm, tn), lambda i, j: (i, fin(j))),
        scratch_shapes=[pltpu.VMEM((nj, 2, SUBLANES, tn), F32), pltpu.VMEM((2, 2, SUBLANES + tm, tn), F32),
                        pltpu.VMEM((tm, d), BF16)],
        compiler_params=_params("arbitrary", "arbitrary"),
        name="ffn_up_conv_glu",
    )(h, w_up, w_up, dw, dw)


def _layer(x, h, layer, p, w, g_next):
    heads = p["b_fgate"].shape[0]
    attn_w = heads * HEAD_DIM
    conv_c = p["conv_b"].shape[0]
    hgrn_w = p["hgrn_norm_g"].shape[0]
    d_ff = w["w_ffn_down"].shape[1]
    cast = lambda name: _cast_bf16(w[name], layer)

    fg0 = 3 * attn_w
    w_in = w["w_in"][layer]
    zq = _matmul(h, _cast_bf16(w["w_in"], layer, 0, fg0), BF16, 1024, 1024, "in_proj_qkv")
    z = _matmul(h, w_in[:, fg0 + heads:].astype(BF16), BF16, 1024, 1024, "in_proj_rest")
    w_fg = jnp.pad(w_in[:, fg0:fg0 + heads], ((0, 0), (0, LANES - heads))).astype(BF16)
    fg = _matmul(h, w_fg, F32, 1024, LANES, "in_proj_fgate")
    hgrn_col = 2 * conv_c
    gate_col = hgrn_col + 4 * hgrn_w

    bias_row = jnp.pad(p["b_fgate"].astype(F32), (0, LANES - heads)).reshape(1, LANES)
    kx = _fgate_bias(fg, bias_row, heads)
    att = _attention(zq, kx, heads, 0, heads, 2 * heads)

    conv = _conv_module(z, 0, conv_c, p["conv_dw"], p["conv_b"], p["conv_ln_g"], p["conv_ln_b"])

    hg = _hgrn(z, hgrn_w // HEAD_DIM, hgrn_col, hgrn_col + hgrn_w, hgrn_col + 2 * hgrn_w, hgrn_col + 3 * hgrn_w,
               p["lower_bound"], p["hgrn_norm_g"])

    merged = _merge(att, conv, hg, cast("w_attn_out"), cast("w_conv_out"), cast("w_hgrn_out"), z, gate_col)
    y = _matmul(merged, cast("w_o"), F32, 1024, 1024, "out_proj")
    x, h = _residual_norm(x, y, p["norm_gains"][1], p["norm_gains"][2])

    act = _ffn_up(h, cast("w_ffn_up"), p["ffn_dw"], d_ff)
    y = _matmul(act, cast("w_ffn_down"), F32, 512, 512, "ffn_down")
    return _residual_norm(x, y, p["norm_gains"][3], g_next)


def kernel(x, norm_gains, w_in, b_fgate, conv_dw, conv_b, conv_ln_g, conv_ln_b, hgrn_lb_logits, hgrn_norm_g,
           w_attn_out, w_conv_out, w_hgrn_out, w_o, w_ffn_up, ffn_dw, w_ffn_down):
    b, s, d = x.shape
    assert b == 1, "the sequence scans assume a single sequence"
    depth = w_in.shape[0]
    p_lb = jax.nn.softmax(hgrn_lb_logits.astype(F32), axis=0)
    lower_bounds = jnp.cumsum(p_lb, axis=0) - p_lb[0]
    weights = dict(w_in=w_in, w_attn_out=w_attn_out, w_conv_out=w_conv_out, w_hgrn_out=w_hgrn_out, w_o=w_o,
                   w_ffn_up=w_ffn_up, w_ffn_down=w_ffn_down)

    xs = x.reshape(s, d)
    h = _rmsnorm_bf16(xs, norm_gains[0, 0])
    for l in range(depth):
        p = dict(norm_gains=norm_gains[l], b_fgate=b_fgate[l], conv_dw=conv_dw[l], conv_b=conv_b[l],
                 conv_ln_g=conv_ln_g[l], conv_ln_b=conv_ln_b[l], lower_bound=lower_bounds[l],
                 hgrn_norm_g=hgrn_norm_g[l], ffn_dw=ffn_dw[l])
        g_next = norm_gains[l + 1, 0] if l + 1 < depth else None
        xs, h = _layer(xs, h, l, p, weights, g_next)
    return xs.reshape(b, s, d)
```

```python
import functools

import jax
import jax.numpy as jnp
from jax import lax
from jax.experimental import pallas as pl
from jax.experimental.pallas import tpu as pltpu

F32 = jnp.float32
BF16 = jnp.bfloat16

HEAD_DIM = 128
LANES = 128
SUBLANES = 8
MXU_DIM = 256
ATTN_SOFTMAX_ROWS = 64
EPS = 1e-6
MASK_VALUE = -1e30
MIN_FORGET = 1e-30
LOG2E = 1.4426950408889634
HGRN_CHUNK = 128
V7X_VMEM_LIMIT_BYTES = 56 * 1024 * 1024

NT_DIMS = (((1,), (1,)), ((), ()))
TN_DIMS = (((0,), (0,)), ((), ()))

ROWS = dict(rmsnorm=512, residual=256, fgate=2048, attention=1024, conv=128, hgrn=1024, merge=1024, ffn_up=1024)


def _params(*semantics):
    return pltpu.CompilerParams(dimension_semantics=semantics, vmem_limit_bytes=V7X_VMEM_LIMIT_BYTES)


def _tile(n, preferred):
    t = preferred
    while t >= 8:
        if n % t == 0:
            return t
        t //= 2
    return n


_dot = functools.partial(jnp.dot, preferred_element_type=F32)


def _rms(x, g):
    return x * lax.rsqrt(jnp.mean(x * x, axis=-1, keepdims=True) + EPS) * g


def _rmsnorm_kernel(x_ref, g_ref, h_ref):
    h_ref[...] = _rms(x_ref[...], g_ref[...]).astype(h_ref.dtype)


def _rmsnorm_bf16(x, g):
    s, d = x.shape
    tm = _tile(s, ROWS["rmsnorm"])
    return pl.pallas_call(
        _rmsnorm_kernel,
        out_shape=jax.ShapeDtypeStruct((s, d), BF16),
        grid=(s // tm,),
        in_specs=[pl.BlockSpec((tm, d), lambda i: (i, 0)), pl.BlockSpec((1, d), lambda i: (0, 0))],
        out_specs=pl.BlockSpec((tm, d), lambda i: (i, 0)),
        compiler_params=_params("parallel"),
        name="rmsnorm",
    )(x, g.reshape(1, d))


def _residual_norm_kernel(x_ref, y_ref, g_post_ref, g_next_ref, x_out_ref, h_ref):
    x_new = x_ref[...] + _rms(y_ref[...], g_post_ref[...])
    x_out_ref[...] = x_new
    h_ref[...] = _rms(x_new, g_next_ref[...]).astype(h_ref.dtype)


def _residual_kernel(x_ref, y_ref, g_post_ref, x_out_ref):
    x_out_ref[...] = x_ref[...] + _rms(y_ref[...], g_post_ref[...])


def _residual_norm(x, y, g_post, g_next):
    s, d = x.shape
    tm = _tile(s, ROWS["residual"])
    row = pl.BlockSpec((tm, d), lambda i: (i, 0))
    vec = pl.BlockSpec((1, d), lambda i: (0, 0))
    x_shape = jax.ShapeDtypeStruct((s, d), F32)
    if g_next is None:
        return pl.pallas_call(
            _residual_kernel, out_shape=x_shape, grid=(s // tm,), in_specs=[row, row, vec], out_specs=row,
            compiler_params=_params("parallel"), name="residual",
        )(x, y, g_post.reshape(1, d)), None
    return pl.pallas_call(
        _residual_norm_kernel,
        out_shape=(x_shape, jax.ShapeDtypeStruct((s, d), BF16)),
        grid=(s // tm,),
        in_specs=[row, row, vec, vec],
        out_specs=(row, row),
        compiler_params=_params("parallel"),
        name="residual_norm",
    )(x, y, g_post.reshape(1, d), g_next.reshape(1, d))


def _matmul_kernel(a_ref, b_ref, o_ref):
    o_ref[...] = _dot(a_ref[...], b_ref[...]).astype(o_ref.dtype)


def _matmul(a, b, out_dtype, tm, tn, name):
    m, k = a.shape
    n = b.shape[1]
    tm, tn = _tile(m, tm), _tile(n, tn)
    return pl.pallas_call(
        _matmul_kernel,
        out_shape=jax.ShapeDtypeStruct((m, n), out_dtype),
        grid=(m // tm, n // tn),
        in_specs=[pl.BlockSpec((tm, k), lambda i, j: (i, 0)), pl.BlockSpec((k, tn), lambda i, j: (0, j))],
        out_specs=pl.BlockSpec((tm, tn), lambda i, j: (i, j)),
        compiler_params=_params("parallel", "parallel"),
        name=name,
    )(a, b)


CAST_BLOCK_BYTES = 4 * 1024 * 1024


def _cast_kernel(w_ref, o_ref):
    o_ref[...] = w_ref[...].astype(o_ref.dtype)


def _cast_bf16(w, layer, col0=0, ncols=None):
    _, k, n = w.shape
    ncols = n - col0 if ncols is None else ncols
    tn = _tile(ncols, 4096)
    while col0 % tn:
        tn //= 2
    assert tn % LANES == 0 and ncols % tn == 0
    tk = _tile(k, max(CAST_BLOCK_BYTES // (4 * tn), 16))
    return pl.pallas_call(
        _cast_kernel,
        out_shape=jax.ShapeDtypeStruct((k, ncols), BF16),
        grid=(k // tk, ncols // tn),
        in_specs=[pl.BlockSpec((None, tk, tn), lambda a, b: (layer, a, col0 // tn + b))],
        out_specs=pl.BlockSpec((tk, tn), lambda a, b: (a, b)),
        compiler_params=_params("parallel", "parallel"),
        name="cast_bf16",
    )(w)


def _cast_split_kernel(a_ref, b_ref, rest_ref, narrow_ref, *, width):
    a = a_ref[...]
    shifted = jnp.concatenate([a[:, width:], b_ref[:, 0:width]], axis=1)
    rest_ref[...] = shifted.astype(rest_ref.dtype)

    @pl.when(pl.program_id(1) == 0)
    def _():
        lane = lax.broadcasted_iota(jnp.int32, narrow_ref.shape, 1)
        narrow_ref[...] = jnp.where(lane < width, a[:, 0:LANES], 0.0).astype(narrow_ref.dtype)


def _cast_split_bf16(w, layer, col0, width):
    _, k, n = w.shape
    ncols = n - col0 - width
    assert 0 < width < LANES and col0 % LANES == 0 and ncols % LANES == 0
    tn = _tile(ncols, 2048)
    while col0 % tn:
        tn //= 2
    tk = _tile(k, max(CAST_BLOCK_BYTES // (4 * tn), 16))
    return pl.pallas_call(
        functools.partial(_cast_split_kernel, width=width),
        out_shape=(jax.ShapeDtypeStruct((k, ncols), BF16), jax.ShapeDtypeStruct((k, LANES), BF16)),
        grid=(k // tk, ncols // tn),
        in_specs=[pl.BlockSpec((None, tk, tn), lambda a, b: (layer, a, col0 // tn + b)),
                  pl.BlockSpec((None, tk, LANES), lambda a, b: (layer, a, (col0 + (b + 1) * tn) // LANES))],
        out_specs=(pl.BlockSpec((tk, tn), lambda a, b: (a, b)), pl.BlockSpec((tk, LANES), lambda a, b: (a, 0))),
        compiler_params=_params("parallel", "arbitrary"),
        name="cast_split_bf16",
    )(w, w)


BF16_TERMS = 3


def _split_bf16(x):
    hi = x.astype(BF16)
    r1 = x - hi.astype(F32)
    mid = r1.astype(BF16)
    lo = (r1 - mid.astype(F32)).astype(BF16)
    return hi, mid, lo


def _lower_tri_bf16(t):
    r = lax.broadcasted_iota(jnp.int32, (t, t), 0)
    c = lax.broadcasted_iota(jnp.int32, (t, t), 1)
    return (c <= r).astype(BF16)


def _tri_cumsum(tri, x):
    hi, mid, lo = _split_bf16(x)
    return _dot(tri, hi) + _dot(tri, mid) + _dot(tri, lo)


def _log_sigmoid(x):
    return jnp.minimum(x, 0.0) - jnp.log1p(jnp.exp(-jnp.abs(x)))


FGATE_SCAN_ROWS = 256


def _fgate_bias_kernel(fg_ref, b_ref, sel_ref, kx_ref, carry_ref):
    @pl.when(pl.program_id(0) == 0)
    def _():
        carry_ref[...] = jnp.zeros_like(carry_ref)

    rows = fg_ref.shape[0]
    t = min(FGATE_SCAN_ROWS, rows)
    tri = _lower_tri_bf16(t)
    carry = carry_ref[...]
    for k in range(rows // t):
        ls = _log_sigmoid(fg_ref[k * t:(k + 1) * t, :] + b_ref[...])
        cs = _tri_cumsum(tri, ls) + carry
        carry = cs[t - 1:t, :]
        terms = jnp.concatenate(_split_bf16(cs * (-LOG2E)), axis=1)
        kx_ref[k * t:(k + 1) * t, :] = _dot(terms, sel_ref[...]).astype(BF16)
    carry_ref[...] = carry


def _fgate_bias(fg, bias_row, heads):
    s, w = fg.shape
    tm = _tile(s, ROWS["fgate"])
    row = jnp.arange(BF16_TERMS * LANES)
    col = jnp.arange(heads * HEAD_DIM)
    sel = ((row[:, None] % LANES == col[None, :] // HEAD_DIM)
           & (row[:, None] // LANES == col[None, :] % HEAD_DIM)).astype(BF16)
    return pl.pallas_call(
        _fgate_bias_kernel,
        out_shape=jax.ShapeDtypeStruct((s, heads * HEAD_DIM), BF16),
        grid=(s // tm,),
        in_specs=[pl.BlockSpec((tm, w), lambda i: (i, 0)), pl.BlockSpec((1, w), lambda i: (0, 0)),
                  pl.BlockSpec(sel.shape, lambda i: (0, 0))],
        out_specs=pl.BlockSpec((tm, heads * HEAD_DIM), lambda i: (i, 0)),
        scratch_shapes=[pltpu.VMEM((1, w), F32)],
        compiler_params=_params("arbitrary"),
        name="fgate_bias",
    )(fg, bias_row, sel)


def _attn_kernel(q_ref, k_ref, v_ref, kx_ref, o_ref, sa_ref, sb_ref, pa_ref, pb_ref, aa_ref, ab_ref, m_ref, acc_ref,
                 ka_ref, *, tile):
    i = pl.program_id(1)
    lane = lax.broadcasted_iota(jnp.int32, (tile, LANES), 1)
    bias_ones = (lane < BF16_TERMS).astype(BF16)
    sum_ones = (lane == 0).astype(BF16)
    q = (q_ref[...].astype(F32) * (HEAD_DIM ** -0.5 * LOG2E)).astype(BF16)
    q_aug = jnp.concatenate([q, bias_ones], axis=1)

    m_ref[...] = jnp.full_like(m_ref, MASK_VALUE)
    acc_ref[...] = jnp.zeros_like(acc_ref)

    def softmax(s_ref, p_ref, a_ref, diagonal=False):
        zeros = []
        for r in range(tile // ATTN_SOFTMAX_ROWS):
            rows = slice(r * ATTN_SOFTMAX_ROWS, (r + 1) * ATTN_SOFTMAX_ROWS)
            s = s_ref[rows, :]
            if diagonal:
                row = lax.broadcasted_iota(jnp.int32, s.shape, 0) + r * ATTN_SOFTMAX_ROWS
                col = lax.broadcasted_iota(jnp.int32, s.shape, 1)
                s = jnp.where(col <= row, s, MASK_VALUE)
            m_prev = m_ref[rows, :]
            m_new = jnp.maximum(m_prev, jnp.max(s, axis=1, keepdims=True))
            a_ref[rows, :] = jnp.exp2(m_prev - m_new)
            p = jnp.exp2(s - m_new).astype(BF16)
            p_ref[rows, :] = p
            m_ref[rows, :] = m_new
            bits = pltpu.bitcast(p[0:2 * SUBLANES, 0:LANES], jnp.uint32)
            zeros.append(pltpu.bitcast((bits >> 16) >> 16, BF16))
        return zeros

    def logits(j, s_ref, gates):
        k0 = pl.multiple_of(j * tile, tile)
        ka_ref[:, 0:HEAD_DIM] = k_ref[pl.ds(k0, tile), :]
        ka_ref[:, HEAD_DIM:2 * HEAD_DIM] = kx_ref[pl.ds(k0, tile), :]
        for n, zero in gates:
            rows = slice(n * MXU_DIM, n * MXU_DIM + 2 * SUBLANES)
            ka_ref[rows, 0:LANES] = ka_ref[rows, 0:LANES] + zero
        s_ref[...] = lax.dot_general(q_aug, ka_ref[...], NT_DIMS, preferred_element_type=F32)

    def accumulate(j, p_ref, a_ref, gates):
        k0 = pl.multiple_of(j * tile, tile)
        for n, zero in gates:
            rows = slice(n * MXU_DIM, n * MXU_DIM + 2 * SUBLANES)
            cols = slice(n * MXU_DIM, n * MXU_DIM + LANES)
            p_ref[rows, cols] = p_ref[rows, cols] + zero
        v_aug = jnp.concatenate([v_ref[pl.ds(k0, tile), :], sum_ones], axis=1)
        acc_ref[...] = a_ref[...] * acc_ref[...] + _dot(p_ref[...], v_aug)

    def stage(soft=None, qk=None, pv=None):
        zeros = softmax(*soft) if soft is not None else []
        blocks = tile // MXU_DIM
        slots = ([("qk", n) for n in range(1, blocks)] if qk is not None else []) + \
                ([("pv", n) for n in range(0 if qk is not None else 1, blocks)] if pv is not None else [])
        gates = dict(qk=[], pv=[])
        if zeros:
            for g, (which, n) in enumerate(slots):
                gates[which].append((n, zeros[max((g + 1) * len(zeros) // (len(slots) + 1) - 1, 0)]))
        if qk is not None:
            logits(*qk, gates["qk"])
        if pv is not None:
            accumulate(*pv, gates["pv"])

    stage(qk=(0, sa_ref))

    @pl.when(i == 0)
    def _():
        stage(soft=(sa_ref, pa_ref, aa_ref, True))
        stage(pv=(0, pa_ref, aa_ref))

    @pl.when(i > 0)
    def _():
        stage(soft=(sa_ref, pa_ref, aa_ref), qk=(1, sb_ref))

    def pair(jj, carry):
        j = 2 * jj + 1
        stage(soft=(sb_ref, pb_ref, ab_ref), qk=(j + 1, sa_ref), pv=(j - 1, pa_ref, aa_ref))
        stage(soft=(sa_ref, pa_ref, aa_ref), qk=(j + 2, sb_ref), pv=(j, pb_ref, ab_ref))
        return carry

    lax.fori_loop(0, jnp.maximum(i - 1, 0) // 2, pair, 0)

    @pl.when(i % 2 == 1)
    def _():
        stage(soft=(sb_ref, pb_ref, ab_ref, True), pv=(i - 1, pa_ref, aa_ref))
        stage(pv=(i, pb_ref, ab_ref))

    @pl.when((i > 0) & (i % 2 == 0))
    def _():
        stage(soft=(sb_ref, pb_ref, ab_ref), qk=(i, sa_ref), pv=(i - 2, pa_ref, aa_ref))
        stage(soft=(sa_ref, pa_ref, aa_ref, True), pv=(i - 1, pb_ref, ab_ref))
        stage(pv=(i, pa_ref, aa_ref))

    o_ref[...] = (acc_ref[:, 0:HEAD_DIM] / acc_ref[:, HEAD_DIM:HEAD_DIM + 1]).astype(o_ref.dtype)


def _attention(zq, kx, heads, q_blk, k_blk, v_blk):
    s = zq.shape[0]
    tile = _tile(s, ROWS["attention"])
    keys = lambda blk: pl.BlockSpec((s, HEAD_DIM), lambda h, i: (0, blk + h))
    return pl.pallas_call(
        functools.partial(_attn_kernel, tile=tile),
        out_shape=jax.ShapeDtypeStruct((s, heads * HEAD_DIM), BF16),
        grid=(heads, s // tile),
        in_specs=[pl.BlockSpec((tile, HEAD_DIM), lambda h, i: (i, q_blk + h)), keys(k_blk), keys(v_blk), keys(0)],
        out_specs=pl.BlockSpec((tile, HEAD_DIM), lambda h, i: (i, h)),
        scratch_shapes=[pltpu.VMEM((tile, tile), F32), pltpu.VMEM((tile, tile), F32),
                        pltpu.VMEM((tile, tile), BF16), pltpu.VMEM((tile, tile), BF16),
                        pltpu.VMEM((tile, 1), F32), pltpu.VMEM((tile, 1), F32),
                        pltpu.VMEM((tile, 1), F32), pltpu.VMEM((tile, 2 * HEAD_DIM), F32),
                        pltpu.VMEM((tile, 2 * HEAD_DIM), BF16)],
        compiler_params=_params("parallel", "arbitrary"),
        name="fox_attention",
    )(zq, zq, zq, kx)


CONV_HALO_ROWS = 32
CONV_ROW_CHUNK = 32


def _conv_kernel(val_ref, gate_ref, pval_ref, pgate_ref, w_ref, b_ref, g_ref, beta_ref, o_ref, u_ref, *, taps):
    tm = val_ref.shape[0]
    halo = pval_ref.shape[0]
    first = pl.program_id(0) == 0
    u_prev = pval_ref[...].astype(F32) * jax.nn.sigmoid(pgate_ref[...].astype(F32))
    u_ref[0, 0:halo, :] = jnp.where(first, 0.0, u_prev)
    u_ref[0, halo:halo + tm, :] = val_ref[...].astype(F32) * jax.nn.sigmoid(gate_ref[...].astype(F32))
    n = halo + tm - SUBLANES
    for sh in range(1, SUBLANES):
        u_ref[sh, 0:n, :] = u_ref[0, sh:sh + n, :]

    rc = min(CONV_ROW_CHUNK, tm)
    for r in range(tm // rc):
        acc = None
        for k in range(taps):
            off = halo - (taps - 1) + k
            lo = r * rc + off - off % SUBLANES
            term = u_ref[off % SUBLANES, lo:lo + rc, :] * w_ref[k:k + 1, :]
            acc = term if acc is None else acc + term
        acc = acc + b_ref[...]
        mu = jnp.mean(acc, axis=-1, keepdims=True)
        xc = acc - mu
        y = xc * lax.rsqrt(jnp.mean(xc * xc, axis=-1, keepdims=True) + EPS) * g_ref[...] + beta_ref[...]
        o_ref[r * rc:(r + 1) * rc, :] = (y * jax.nn.sigmoid(y)).astype(o_ref.dtype)


def _conv_module(z, glu_col, channels, w, b, ln_g, ln_b):
    s = z.shape[0]
    taps = w.shape[0]
    assert taps - 1 <= CONV_HALO_ROWS and glu_col % channels == 0
    tm = _tile(s, ROWS["conv"])
    assert tm % CONV_HALO_ROWS == 0
    vblk = glu_col // channels
    per = tm // CONV_HALO_ROWS
    cur = lambda off: pl.BlockSpec((tm, channels), lambda i: (i, vblk + off))
    prev = lambda off: pl.BlockSpec((CONV_HALO_ROWS, channels), lambda i: (jnp.maximum(i * per - 1, 0), vblk + off))
    vec = pl.BlockSpec((1, channels), lambda i: (0, 0))
    return pl.pallas_call(
        functools.partial(_conv_kernel, taps=taps),
        out_shape=jax.ShapeDtypeStruct((s, channels), BF16),
        grid=(s // tm,),
        in_specs=[cur(0), cur(1), prev(0), prev(1), pl.BlockSpec((taps, channels), lambda i: (0, 0)), vec, vec, vec],
        out_specs=pl.BlockSpec((tm, channels), lambda i: (i, 0)),
        scratch_shapes=[pltpu.VMEM((SUBLANES, CONV_HALO_ROWS + tm, channels), F32)],
        compiler_params=_params("parallel"),
        name="conformer_conv",
    )(z, z, z, z, w, b.reshape(1, -1), ln_g.reshape(1, -1), ln_b.reshape(1, -1))


HGRN_HEADS_PER_STEP = 2


def _hgrn_kernel(q_ref, f_ref, i_ref, g_ref, lb_ref, gain_ref, o_ref, st_ref, *, group):
    @pl.when(pl.program_id(1) == 0)
    def _():
        st_ref[...] = jnp.zeros_like(st_ref)

    rows = q_ref.shape[0]
    c = min(HGRN_CHUNK, rows)
    levels = c.bit_length() - 1
    assert 1 << levels == c
    tri = _lower_tri_bf16(c)
    t_idx = lax.broadcasted_iota(jnp.int32, (c, c), 0)
    s_idx = lax.broadcasted_iota(jnp.int32, (c, c), 1)
    x = t_idx ^ s_idx
    split_level = jnp.full((c, c), -1, jnp.int32)
    for lvl in range(levels):
        split_level = split_level + (x >= (1 << lvl)).astype(jnp.int32)
    split_level = jnp.where(t_idx > s_idx, split_level, -1)
    row_idx = lax.broadcasted_iota(jnp.int32, (c, HEAD_DIM), 0)

    def head_chunk(r0, hh):
        cols = slice(hh * HEAD_DIM, (hh + 1) * HEAD_DIM)
        lb = lb_ref[:, cols]
        zf = f_ref[pl.ds(r0, c), cols].astype(F32)
        forget = lb + (1.0 - lb) * jax.nn.sigmoid(zf)
        log_f = jnp.log(jnp.maximum(forget, MIN_FORGET))
        key = (1.0 - lb) * jax.nn.sigmoid(-zf)
        hq = q_ref[pl.ds(r0, c), cols].astype(F32)
        query = hq * jax.nn.sigmoid(hq) * (HEAD_DIM ** -0.5)
        val = i_ref[pl.ds(r0, c), cols]
        cum = _tri_cumsum(tri, log_f)

        st = st_ref[hh]
        o = lax.dot_general((query * jnp.exp(cum)).astype(BF16), st.astype(BF16), NT_DIMS,
                            preferred_element_type=F32)

        scores = jnp.zeros((c, c), F32)
        seg_end = cum
        for lvl in range(levels):
            h = 1 << lvl
            odd = ((row_idx >> lvl) & 1) == 1
            prev_end = pltpu.roll(seg_end, h, axis=0)
            w = jnp.where(odd, query, key) * jnp.exp(jnp.where(odd, cum - prev_end, seg_end - cum))
            w = w.astype(BF16)
            part = lax.dot_general(w, w, NT_DIMS, preferred_element_type=F32)
            scores = scores + jnp.where(split_level == lvl, part, 0.0)
            seg_end = jnp.where(odd, seg_end, pltpu.roll(seg_end, c - h, axis=0))
        o = o + _dot(scores.astype(BF16), val)
        o = o + jnp.sum(query * key, axis=1, keepdims=True) * val.astype(F32)

        cum_last = cum[c - 1:c, :]
        k_dec = (key * jnp.exp(cum_last - cum)).astype(BF16)
        st_ref[hh] = st * jnp.exp(cum_last) + lax.dot_general(val, k_dec, TN_DIMS, preferred_element_type=F32)

        o = _rms(o, gain_ref[:, cols])
        hg = g_ref[pl.ds(r0, c), cols].astype(F32)
        o_ref[pl.ds(r0, c), cols] = (o * (hg * jax.nn.sigmoid(hg))).astype(o_ref.dtype)

    def step(ci, carry):
        r0 = pl.multiple_of(ci * c, c)
        for hh in range(group):
            head_chunk(r0, hh)
        return carry

    lax.fori_loop(0, rows // c, step, 0)


def _hgrn(z, heads, q_col, f_col, i_col, g_col, lower_bound, gain):
    s = z.shape[0]
    tm = _tile(s, ROWS["hgrn"])
    group =HGRN_HEADS_PER_STEP if heads % HGRN_HEADS_PER_STEP == 0 else 1
    width = group * HEAD_DIM
    assert all(c % width == 0 for c in (q_col, f_col, i_col, g_col))
    col = lambda c0: pl.BlockSpec((tm, width), lambda h, i: (i, c0 // width + h))
    vec = pl.BlockSpec((1, width), lambda h, i: (0, h))
    return pl.pallas_call(
        functools.partial(_hgrn_kernel, group=group),
        out_shape=jax.ShapeDtypeStruct((s, heads * HEAD_DIM), BF16),
        grid=(heads // group, s // tm),
        in_specs=[col(q_col), col(f_col), col(i_col), col(g_col), vec, vec],
        out_specs=pl.BlockSpec((tm, width), lambda h, i: (i, h)),
        scratch_shapes=[pltpu.VMEM((group, HEAD_DIM, HEAD_DIM), F32)],
        compiler_params=_params("parallel", "arbitrary"),
        name="hgrn2",
    )(z, z, z, z, lower_bound.reshape(1, -1), gain.reshape(1, -1))


def _merge_kernel(a_ref, b_ref, c_ref, wa_ref, wb_ref, wc_ref, ga_ref, gb_ref, gc_ref, o_ref):
    gate = lambda ref: jax.nn.sigmoid(ref[...].astype(F32))
    merged = gate(ga_ref) * _dot(a_ref[...], wa_ref[...])
    merged = merged + gate(gb_ref) * _dot(b_ref[...], wb_ref[...])
    merged = merged + gate(gc_ref) * _dot(c_ref[...], wc_ref[...])
    o_ref[...] = merged.astype(o_ref.dtype)


def _merge(att, conv, hgrn, wa, wb, wc, z, gate_col):
    s = att.shape[0]
    d = wa.shape[1]
    tm, tn = _tile(s, ROWS["merge"]), _tile(d, 512)
    assert gate_col % tn == 0 and d % tn == 0
    act = lambda a: pl.BlockSpec((tm, a.shape[1]), lambda i, j: (i, 0))
    wgt = lambda w: pl.BlockSpec((w.shape[0], tn), lambda i, j: (0, j))
    gate = lambda n: pl.BlockSpec((tm, tn), lambda i, j: (i, (gate_col + n * d) // tn + j))
    return pl.pallas_call(
        _merge_kernel,
        out_shape=jax.ShapeDtypeStruct((s, d), BF16),
        grid=(s // tm, d // tn),
        in_specs=[act(att), act(conv), act(hgrn), wgt(wa), wgt(wb), wgt(wc), gate(0), gate(1), gate(2)],
        out_specs=pl.BlockSpec((tm, tn), lambda i, j: (i, j)),
        compiler_params=_params("parallel", "parallel"),
        name="branch_merge",
    )(att, conv, hgrn, wa, wb, wc, z, z, z)


FFN_ROW_CHUNK = 64


def _ffn_up_kernel(h_ref, wg_ref, wu_ref, dwg_ref, dwu_ref, o_ref, carry_ref, ext_ref, lhs_ref, *, taps, nj, rc):
    i = pl.program_id(0)
    j = pl.program_id(1)
    tm = h_ref.shape[0]
    halo = SUBLANES
    jp = jnp.maximum(j - 1, 0)

    @pl.when((i == 0) & (j < nj))
    def _():
        carry_ref[j] = jnp.zeros(carry_ref.shape[1:], F32)

    @pl.when(j == 0)
    def _():
        lhs_ref[...] = h_ref[...]

    def finish_halo(slot):
        for n in range(2):
            ext_ref[slot, n, 0:halo, :] = carry_ref[jp, n]
            carry_ref[jp, n] = ext_ref[slot, n, tm:tm + halo, :]

    def finish(slot, r):
        lo = halo - (taps - 1) + r * rc
        conv = []
        for n, dw_ref in enumerate((dwg_ref, dwu_ref)):
            acc = ext_ref[slot, n, lo:lo + rc, :] * dw_ref[0:1, :]
            for k in range(1, taps):
                acc = acc + ext_ref[slot, n, lo + k:lo + k + rc, :] * dw_ref[k:k + 1, :]
            conv.append(acc)
        gate, up = conv
        out = (gate * jax.nn.sigmoid(gate) * up).astype(o_ref.dtype)
        o_ref[r * rc:(r + 1) * rc, :] = out
        return out

    def pace(r, out):
        nchunks = tm // rc
        bits = pltpu.bitcast(out[0:2 * SUBLANES, 0:LANES], jnp.uint32)
        zero = pltpu.bitcast((bits >> 16) >> 16, BF16)
        row0 = (tm // nchunks) * r
        col0 = ((lhs_ref.shape[1] // LANES) * r // nchunks) * LANES
        rows, cols = slice(row0, row0 + 2 * SUBLANES), slice(col0, col0 + LANES)
        lhs_ref[rows, cols] = lhs_ref[rows, cols] + zero

    def step(project_slot, finish_slot):
        if finish_slot is not None:
            finish_halo(finish_slot)
            for r in range(tm // rc):
                out = finish(finish_slot, r)
                if project_slot is not None:
                    pace(r, out)
        if project_slot is not None:
            ext_ref[project_slot, 0, halo:halo + tm, :] = _dot(lhs_ref[...], wg_ref[...])
            ext_ref[project_slot, 1, halo:halo + tm, :] = _dot(lhs_ref[...], wu_ref[...])

    interior = (j > 0) & (j < nj)
    pl.when(j == 0)(lambda: step(0, None))
    pl.when(interior & (j % 2 == 0))(lambda: step(0, 1))
    pl.when(interior & (j % 2 == 1))(lambda: step(1, 0))
    pl.when(j == nj)(lambda: step(None, (nj - 1) % 2))


def _ffn_up(h, w_up, dw, d_ff):
    s, d = h.shape
    taps = dw.shape[0]
    assert taps - 1 <= SUBLANES
    tm, tn = _tile(s, ROWS["ffn_up"]), _tile(d_ff, 256)
    nj = d_ff // tn
    cur = lambda j: jnp.minimum(j, nj - 1)
    fin = lambda j: jnp.maximum(j - 1, 0)
    return pl.pallas_call(
        functools.partial(_ffn_up_kernel, taps=taps, nj=nj, rc=_tile(tm, FFN_ROW_CHUNK)),
        out_shape=jax.ShapeDtypeStruct((s, d_ff), BF16),
        grid=(s // tm, nj + 1),
        in_specs=[
            pl.BlockSpec((tm, d), lambda i, j: (i, 0)),
            pl.BlockSpec((d, tn), lambda i, j: (0, cur(j))),
            pl.BlockSpec((d, tn), lambda i, j: (0, nj + cur(j))),
            pl.BlockSpec((taps, tn), lambda i, j: (0, fin(j))),
            pl.BlockSpec((taps, tn), lambda i, j: (0, nj + fin(j))),
        ],
        out_specs=pl.BlockSpec((tm, tn), lambda i, j: (i, fin(j))),
        scratch_shapes=[pltpu.VMEM((nj, 2, SUBLANES, tn), F32), pltpu.VMEM((2, 2, SUBLANES + tm, tn), F32),
                        pltpu.VMEM((tm, d), BF16)],
        compiler_params=_params("arbitrary", "arbitrary"),
        name="ffn_up_conv_glu",
    )(h, w_up, w_up, dw, dw)


def _layer(x, h, layer, p, w, g_next):
    heads = p["b_fgate"].shape[0]
    attn_w = heads * HEAD_DIM
    conv_c = p["conv_b"].shape[0]
    hgrn_w = p["hgrn_norm_g"].shape[0]
    d_ff = w["w_ffn_down"].shape[1]
    cast = lambda name: _cast_bf16(w[name], layer)

    fg0 = 3 * attn_w
    w_rest, w_fg = _cast_split_bf16(w["w_in"], layer, fg0, heads)
    zq = _matmul(h, _cast_bf16(w["w_in"], layer, 0, fg0), BF16, 1024, 1024, "in_proj_qkv")
    z = _matmul(h, w_rest, BF16, 1024, 1024, "in_proj_rest")
    fg = _matmul(h, w_fg, F32, 1024, LANES, "in_proj_fgate")
    hgrn_col = 2 * conv_c
    gate_col = hgrn_col + 4 * hgrn_w

    bias_row = jnp.pad(p["b_fgate"].astype(F32), (0, LANES - heads)).reshape(1, LANES)
    kx = _fgate_bias(fg, bias_row, heads)
    att = _attention(zq, kx, heads, 0, heads, 2 * heads)

    conv = _conv_module(z, 0, conv_c, p["conv_dw"], p["conv_b"], p["conv_ln_g"], p["conv_ln_b"])

    hg = _hgrn(z, hgrn_w // HEAD_DIM, hgrn_col, hgrn_col + hgrn_w, hgrn_col + 2 * hgrn_w, hgrn_col + 3 * hgrn_w,
               p["lower_bound"], p["hgrn_norm_g"])

    merged = _merge(att, conv, hg, cast("w_attn_out"), cast("w_conv_out"), cast("w_hgrn_out"), z, gate_col)
    y = _matmul(merged, cast("w_o"), F32, 1024, 1024, "out_proj")
    x, h = _residual_norm(x, y, p["norm_gains"][1], p["norm_gains"][2])

    act = _ffn_up(h, cast("w_ffn_up"), p["ffn_dw"], d_ff)
    y = _matmul(act, cast("w_ffn_down"), F32, 512, 512, "ffn_down")
    return _residual_norm(x, y, p["norm_gains"][3], g_next)


def kernel(x, norm_gains, w_in, b_fgate, conv_dw, conv_b, conv_ln_g, conv_ln_b, hgrn_lb_logits, hgrn_norm_g,
           w_attn_out, w_conv_out, w_hgrn_out, w_o, w_ffn_up, ffn_dw, w_ffn_down):
    b, s, d = x.shape
    assert b == 1, "the sequence scans assume a single sequence"
    depth = w_in.shape[0]
    p_lb = jax.nn.softmax(hgrn_lb_logits.astype(F32), axis=0)
    lower_bounds = jnp.cumsum(p_lb, axis=0) - p_lb[0]
    weights = dict(w_in=w_in, w_attn_out=w_attn_out, w_conv_out=w_conv_out, w_hgrn_out=w_hgrn_out, w_o=w_o,
                   w_ffn_up=w_ffn_up, w_ffn_down=w_ffn_down)

    xs = x.reshape(s, d)
    h = _rmsnorm_bf16(xs, norm_gains[0, 0])
    for l in range(depth):
        p = dict(norm_gains=norm_gains[l], b_fgate=b_fgate[l], conv_dw=conv_dw[l], conv_b=conv_b[l],
                 conv_ln_g=conv_ln_g[l], conv_ln_b=conv_ln_b[l], lower_bound=lower_bounds[l],
                 hgrn_norm_g=hgrn_norm_g[l], ffn_dw=ffn_dw[l])
        g_next = norm_gains[l + 1, 0] if l + 1 < depth else None
        xs, h = _layer(xs, h, l, p, weights, g_next)
    return xs.reshape(b, s, d)
```

```python
import functools

import jax
import jax.numpy as jnp
from jax import lax
from jax.experimental import pallas as pl
from jax.experimental.pallas import tpu as pltpu

F32 = jnp.float32
BF16 = jnp.bfloat16

HEAD_DIM = 128
LANES = 128
SUBLANES = 8
MXU_DIM = 256
ATTN_SOFTMAX_ROWS = 64
EPS = 1e-6
MASK_VALUE = -1e30
MIN_FORGET = 1e-30
LOG2E = 1.4426950408889634
HGRN_CHUNK = 128
V7X_VMEM_LIMIT_BYTES = 56 * 1024 * 1024

NT_DIMS = (((1,), (1,)), ((), ()))
TN_DIMS = (((0,), (0,)), ((), ()))

ROWS = dict(rmsnorm=512, residual=256, fgate=2048, attention=1024, conv=128, hgrn=1024, merge=1024, ffn_up=1024)


def _params(*semantics):
    return pltpu.CompilerParams(dimension_semantics=semantics, vmem_limit_bytes=V7X_VMEM_LIMIT_BYTES)


def _tile(n, preferred):
    t = preferred
    while t >= 8:
        if n % t == 0:
            return t
        t //= 2
    return n


_dot = functools.partial(jnp.dot, preferred_element_type=F32)


def _rms(x, g):
    return x * lax.rsqrt(jnp.mean(x * x, axis=-1, keepdims=True) + EPS) * g


def _rmsnorm_kernel(x_ref, g_ref, h_ref):
    h_ref[...] = _rms(x_ref[...], g_ref[...]).astype(h_ref.dtype)


def _rmsnorm_bf16(x, g):
    s, d = x.shape
    tm = _tile(s, ROWS["rmsnorm"])
    return pl.pallas_call(
        _rmsnorm_kernel,
        out_shape=jax.ShapeDtypeStruct((s, d), BF16),
        grid=(s // tm,),
        in_specs=[pl.BlockSpec((tm, d), lambda i: (i, 0)), pl.BlockSpec((1, d), lambda i: (0, 0))],
        out_specs=pl.BlockSpec((tm, d), lambda i: (i, 0)),
        compiler_params=_params("parallel"),
        name="rmsnorm",
    )(x, g.reshape(1, d))


def _residual_norm_kernel(x_ref, y_ref, g_post_ref, g_next_ref, x_out_ref, h_ref):
    x_new = x_ref[...] + _rms(y_ref[...], g_post_ref[...])
    x_out_ref[...] = x_new
    h_ref[...] = _rms(x_new, g_next_ref[...]).astype(h_ref.dtype)


def _residual_kernel(x_ref, y_ref, g_post_ref, x_out_ref):
    x_out_ref[...] = x_ref[...] + _rms(y_ref[...], g_post_ref[...])


def _residual_norm(x, y, g_post, g_next):
    s, d = x.shape
    tm = _tile(s, ROWS["residual"])
    row = pl.BlockSpec((tm, d), lambda i: (i, 0))
    vec = pl.BlockSpec((1, d), lambda i: (0, 0))
    x_shape = jax.ShapeDtypeStruct((s, d), F32)
    if g_next is None:
        return pl.pallas_call(
            _residual_kernel, out_shape=x_shape, grid=(s // tm,), in_specs=[row, row, vec], out_specs=row,
            compiler_params=_params("parallel"), name="residual",
        )(x, y, g_post.reshape(1, d)), None
    return pl.pallas_call(
        _residual_norm_kernel,
        out_shape=(x_shape, jax.ShapeDtypeStruct((s, d), BF16)),
        grid=(s // tm,),
        in_specs=[row, row, vec, vec],
        out_specs=(row, row),
        compiler_params=_params("parallel"),
        name="residual_norm",
    )(x, y, g_post.reshape(1, d), g_next.reshape(1, d))


def _matmul_kernel(a_ref, b_ref, o_ref):
    o_ref[...] = _dot(a_ref[...], b_ref[...]).astype(o_ref.dtype)


def _matmul(a, b, out_dtype, tm, tn, name, layer=None, col0=0, ncols=None):
    m, k = a.shape
    n = (b.shape[-1] - col0) if ncols is None else ncols
    tm, tn = _tile(m, tm), _tile(n, tn)
    while col0 % tn:
        tn //= 2
    assert tn % LANES == 0
    if b.ndim == 3:
        b_spec = pl.BlockSpec((None, k, tn), lambda i, j: (layer, 0, col0 // tn + j))
    else:
        b_spec = pl.BlockSpec((k, tn), lambda i, j: (0, col0 // tn + j))
    return pl.pallas_call(
        _matmul_kernel,
        out_shape=jax.ShapeDtypeStruct((m, n), out_dtype),
        grid=(m // tm, n // tn),
        in_specs=[pl.BlockSpec((tm, k), lambda i, j: (i, 0)), b_spec],
        out_specs=pl.BlockSpec((tm, tn), lambda i, j: (i, j)),
        compiler_params=_params("parallel", "parallel"),
        name=name,
    )(a, b)


CAST_BLOCK_BYTES = 4 * 1024 * 1024


def _cast_kernel(w_ref, o_ref):
    o_ref[...] = w_ref[...].astype(o_ref.dtype)


def _cast_bf16(w, layer, col0=0, ncols=None):
    _, k, n = w.shape
    ncols = n - col0 if ncols is None else ncols
    tn = _tile(ncols, 4096)
    while col0 % tn:
        tn //= 2
    assert tn % LANES == 0 and ncols % tn == 0
    tk = _tile(k, max(CAST_BLOCK_BYTES // (4 * tn), 16))
    return pl.pallas_call(
        _cast_kernel,
        out_shape=jax.ShapeDtypeStruct((k, ncols), BF16),
        grid=(k // tk, ncols // tn),
        in_specs=[pl.BlockSpec((None, tk, tn), lambda a, b: (layer, a, col0 // tn + b))],
        out_specs=pl.BlockSpec((tk, tn), lambda a, b: (a, b)),
        compiler_params=_params("parallel", "parallel"),
        name="cast_bf16",
    )(w)


def _reorder_w_in(w_in, heads):
    fg0 = 3 * heads * HEAD_DIM
    assert heads <= LANES
    narrow = jnp.pad(w_in[:, :, fg0:fg0 + heads], ((0, 0), (0, 0), (0, LANES - heads)))
    return jnp.concatenate([w_in[:, :, :fg0], w_in[:, :, fg0 + heads:], narrow], axis=2).astype(BF16)


BF16_TERMS = 3


def _split_bf16(x):
    hi = x.astype(BF16)
    r1 = x - hi.astype(F32)
    mid = r1.astype(BF16)
    lo = (r1 - mid.astype(F32)).astype(BF16)
    return hi, mid, lo


def _lower_tri_bf16(t):
    r = lax.broadcasted_iota(jnp.int32, (t, t), 0)
    c = lax.broadcasted_iota(jnp.int32, (t, t), 1)
    return (c <= r).astype(BF16)


def _tri_cumsum(tri, x):
    hi, mid, lo = _split_bf16(x)
    return _dot(tri, hi) + _dot(tri, mid) + _dot(tri, lo)


def _log_sigmoid(x):
    return jnp.minimum(x, 0.0) - jnp.log1p(jnp.exp(-jnp.abs(x)))


FGATE_SCAN_ROWS = 256


def _fgate_bias_kernel(fg_ref, b_ref, sel_ref, kx_ref, carry_ref):
    @pl.when(pl.program_id(0) == 0)
    def _():
        carry_ref[...] = jnp.zeros_like(carry_ref)

    rows = fg_ref.shape[0]
    t = min(FGATE_SCAN_ROWS, rows)
    tri = _lower_tri_bf16(t)
    carry = carry_ref[...]
    for k in range(rows // t):
        ls = _log_sigmoid(fg_ref[k * t:(k + 1) * t, :] + b_ref[...])
        cs = _tri_cumsum(tri, ls) + carry
        carry = cs[t - 1:t, :]
        terms = jnp.concatenate(_split_bf16(cs * (-LOG2E)), axis=1)
        kx_ref[k * t:(k + 1) * t, :] = _dot(terms, sel_ref[...]).astype(BF16)
    carry_ref[...] = carry


def _fgate_bias(fg, bias_row, heads):
    s, w = fg.shape
    tm = _tile(s, ROWS["fgate"])
    row = jnp.arange(BF16_TERMS * LANES)
    col = jnp.arange(heads * HEAD_DIM)
    sel = ((row[:, None] % LANES == col[None, :] // HEAD_DIM)
           & (row[:, None] // LANES == col[None, :] % HEAD_DIM)).astype(BF16)
    return pl.pallas_call(
        _fgate_bias_kernel,
        out_shape=jax.ShapeDtypeStruct((s, heads * HEAD_DIM), BF16),
        grid=(s // tm,),
        in_specs=[pl.BlockSpec((tm, w), lambda i: (i, 0)), pl.BlockSpec((1, w), lambda i: (0, 0)),
                  pl.BlockSpec(sel.shape, lambda i: (0, 0))],
        out_specs=pl.BlockSpec((tm, heads * HEAD_DIM), lambda i: (i, 0)),
        scratch_shapes=[pltpu.VMEM((1, w), F32)],
        compiler_params=_params("arbitrary"),
        name="fgate_bias",
    )(fg, bias_row, sel)


def _attn_kernel(q_ref, k_ref, v_ref, kx_ref, o_ref, sa_ref, sb_ref, pa_ref, pb_ref, aa_ref, ab_ref, m_ref, acc_ref,
                 ka_ref, *, tile):
    i = pl.program_id(1)
    lane = lax.broadcasted_iota(jnp.int32, (tile, LANES), 1)
    bias_ones = (lane < BF16_TERMS).astype(BF16)
    sum_ones = (lane == 0).astype(BF16)
    q = (q_ref[...].astype(F32) * (HEAD_DIM ** -0.5 * LOG2E)).astype(BF16)
    q_aug = jnp.concatenate([q, bias_ones], axis=1)

    m_ref[...] = jnp.full_like(m_ref, MASK_VALUE)
    acc_ref[...] = jnp.zeros_like(acc_ref)

    def softmax(s_ref, p_ref, a_ref, diagonal=False):
        zeros = []
        for r in range(tile // ATTN_SOFTMAX_ROWS):
            rows = slice(r * ATTN_SOFTMAX_ROWS, (r + 1) * ATTN_SOFTMAX_ROWS)
            s = s_ref[rows, :]
            if diagonal:
                row = lax.broadcasted_iota(jnp.int32, s.shape, 0) + r * ATTN_SOFTMAX_ROWS
                col = lax.broadcasted_iota(jnp.int32, s.shape, 1)
                s = jnp.where(col <= row, s, MASK_VALUE)
            m_prev = m_ref[rows, :]
            m_new = jnp.maximum(m_prev, jnp.max(s, axis=1, keepdims=True))
            a_ref[rows, :] = jnp.exp2(m_prev - m_new)
            p = jnp.exp2(s - m_new).astype(BF16)
            p_ref[rows, :] = p
            m_ref[rows, :] = m_new
            bits = pltpu.bitcast(p[0:2 * SUBLANES, 0:LANES], jnp.uint32)
            zeros.append(pltpu.bitcast((bits >> 16) >> 16, BF16))
        return zeros

    def logits(j, s_ref, gates):
        k0 = pl.multiple_of(j * tile, tile)
        ka_ref[:, 0:HEAD_DIM] = k_ref[pl.ds(k0, tile), :]
        ka_ref[:, HEAD_DIM:2 * HEAD_DIM] = kx_ref[pl.ds(k0, tile), :]
        for n, zero in gates:
            rows = slice(n * MXU_DIM, n * MXU_DIM + 2 * SUBLANES)
            ka_ref[rows, 0:LANES] = ka_ref[rows, 0:LANES] + zero
        s_ref[...] = lax.dot_general(q_aug, ka_ref[...], NT_DIMS, preferred_element_type=F32)

    def accumulate(j, p_ref, a_ref, gates):
        k0 = pl.multiple_of(j * tile, tile)
        for n, zero in gates:
            rows = slice(n * MXU_DIM, n * MXU_DIM + 2 * SUBLANES)
            cols = slice(n * MXU_DIM, n * MXU_DIM + LANES)
            p_ref[rows, cols] = p_ref[rows, cols] + zero
        v_aug = jnp.concatenate([v_ref[pl.ds(k0, tile), :], sum_ones], axis=1)
        acc_ref[...] = a_ref[...] * acc_ref[...] + _dot(p_ref[...], v_aug)

    def stage(soft=None, qk=None, pv=None):
        zeros = softmax(*soft) if soft is not None else []
        blocks = tile // MXU_DIM
        slots = ([("qk", n) for n in range(1, blocks)] if qk is not None else []) + \
                ([("pv", n) for n in range(0 if qk is not None else 1, blocks)] if pv is not None else [])
        gates = dict(qk=[], pv=[])
        if zeros:
            for g, (which, n) in enumerate(slots):
                gates[which].append((n, zeros[max((g + 1) * len(zeros) // (len(slots) + 1) - 1, 0)]))
        if qk is not None:
            logits(*qk, gates["qk"])
        if pv is not None:
            accumulate(*pv, gates["pv"])

    stage(qk=(0, sa_ref))

    @pl.when(i == 0)
    def _():
        stage(soft=(sa_ref, pa_ref, aa_ref, True))
        stage(pv=(0, pa_ref, aa_ref))

    @pl.when(i > 0)
    def _():
        stage(soft=(sa_ref, pa_ref, aa_ref), qk=(1, sb_ref))

    def pair(jj, carry):
        j = 2 * jj + 1
        stage(soft=(sb_ref, pb_ref, ab_ref), qk=(j + 1, sa_ref), pv=(j - 1, pa_ref, aa_ref))
        stage(soft=(sa_ref, pa_ref, aa_ref), qk=(j + 2, sb_ref), pv=(j, pb_ref, ab_ref))
        return carry

    lax.fori_loop(0, jnp.maximum(i - 1, 0) // 2, pair, 0)

    @pl.when(i % 2 == 1)
    def _():
        stage(soft=(sb_ref, pb_ref, ab_ref, True), pv=(i - 1, pa_ref, aa_ref))
        stage(pv=(i, pb_ref, ab_ref))

    @pl.when((i > 0) & (i % 2 == 0))
    def _():
        stage(soft=(sb_ref, pb_ref, ab_ref), qk=(i, sa_ref), pv=(i - 2, pa_ref, aa_ref))
        stage(soft=(sa_ref, pa_ref, aa_ref, True), pv=(i - 1, pb_ref, ab_ref))
        stage(pv=(i, pa_ref, aa_ref))

    o_ref[...] = (acc_ref[:, 0:HEAD_DIM] / acc_ref[:, HEAD_DIM:HEAD_DIM + 1]).astype(o_ref.dtype)


def _attention(zq, kx, heads, q_blk, k_blk, v_blk):
    s = zq.shape[0]
    tile = _tile(s, ROWS["attention"])
    keys = lambda blk: pl.BlockSpec((s, HEAD_DIM), lambda h, i: (0, blk + h))
    return pl.pallas_call(
        functools.partial(_attn_kernel, tile=tile),
        out_shape=jax.ShapeDtypeStruct((s, heads * HEAD_DIM), BF16),
        grid=(heads, s // tile),
        in_specs=[pl.BlockSpec((tile, HEAD_DIM), lambda h, i: (i, q_blk + h)), keys(k_blk), keys(v_blk), keys(0)],
        out_specs=pl.BlockSpec((tile, HEAD_DIM), lambda h, i: (i, h)),
        scratch_shapes=[pltpu.VMEM((tile, tile), F32), pltpu.VMEM((tile, tile), F32),
                        pltpu.VMEM((tile, tile), BF16), pltpu.VMEM((tile, tile), BF16),
                        pltpu.VMEM((tile, 1), F32), pltpu.VMEM((tile, 1), F32),
                        pltpu.VMEM((tile, 1), F32), pltpu.VMEM((tile, 2 * HEAD_DIM), F32),
                        pltpu.VMEM((tile, 2 * HEAD_DIM), BF16)],
        compiler_params=_params("parallel", "arbitrary"),
        name="fox_attention",
    )(zq, zq, zq, kx)


CONV_HALO_ROWS = 32
CONV_ROW_CHUNK = 32


def _conv_kernel(val_ref, gate_ref, pval_ref, pgate_ref, w_ref, b_ref, g_ref, beta_ref, o_ref, u_ref, *, taps):
    tm = val_ref.shape[0]
    halo = pval_ref.shape[0]
    first = pl.program_id(0) == 0
    u_prev = pval_ref[...].astype(F32) * jax.nn.sigmoid(pgate_ref[...].astype(F32))
    u_ref[0, 0:halo, :] = jnp.where(first, 0.0, u_prev)
    u_ref[0, halo:halo + tm, :] = val_ref[...].astype(F32) * jax.nn.sigmoid(gate_ref[...].astype(F32))
    n = halo + tm - SUBLANES
    for sh in range(1, SUBLANES):
        u_ref[sh, 0:n, :] = u_ref[0, sh:sh + n, :]

    rc = min(CONV_ROW_CHUNK, tm)
    for r in range(tm // rc):
        acc = None
        for k in range(taps):
            off = halo - (taps - 1) + k
            lo = r * rc + off - off % SUBLANES
            term = u_ref[off % SUBLANES, lo:lo + rc, :] * w_ref[k:k + 1, :]
            acc = term if acc is None else acc + term
        acc = acc + b_ref[...]
        mu = jnp.mean(acc, axis=-1, keepdims=True)
        xc = acc - mu
        y = xc * lax.rsqrt(jnp.mean(xc * xc, axis=-1, keepdims=True) + EPS) * g_ref[...] + beta_ref[...]
        o_ref[r * rc:(r + 1) * rc, :] = (y * jax.nn.sigmoid(y)).astype(o_ref.dtype)


def _conv_module(z, glu_col, channels, w, b, ln_g, ln_b):
    s = z.shape[0]
    taps = w.shape[0]
    assert taps - 1 <= CONV_HALO_ROWS and glu_col % channels == 0
    tm = _tile(s, ROWS["conv"])
    assert tm % CONV_HALO_ROWS == 0
    vblk = glu_col // channels
    per = tm // CONV_HALO_ROWS
    cur = lambda off: pl.BlockSpec((tm, channels), lambda i: (i, vblk + off))
    prev = lambda off: pl.BlockSpec((CONV_HALO_ROWS, channels), lambda i: (jnp.maximum(i * per - 1, 0), vblk + off))
    vec = pl.BlockSpec((1, channels), lambda i: (0, 0))
    return pl.pallas_call(
        functools.partial(_conv_kernel, taps=taps),
        out_shape=jax.ShapeDtypeStruct((s, channels), BF16),
        grid=(s // tm,),
        in_specs=[cur(0), cur(1), prev(0), prev(1), pl.BlockSpec((taps, channels), lambda i: (0, 0)), vec, vec, vec],
        out_specs=pl.BlockSpec((tm, channels), lambda i: (i, 0)),
        scratch_shapes=[pltpu.VMEM((SUBLANES, CONV_HALO_ROWS + tm, channels), F32)],
        compiler_params=_params("parallel"),
        name="conformer_conv",
    )(z, z, z, z, w, b.reshape(1, -1), ln_g.reshape(1, -1), ln_b.reshape(1, -1))


HGRN_HEADS_PER_STEP = 2


def _hgrn_kernel(q_ref, f_ref, i_ref, g_ref, lb_ref, gain_ref, o_ref, st_ref, *, group):
    @pl.when(pl.program_id(1) == 0)
    def _():
        st_ref[...] = jnp.zeros_like(st_ref)

    rows = q_ref.shape[0]
    c = min(HGRN_CHUNK, rows)
    levels = c.bit_length() - 1
    assert 1 << levels == c
    tri = _lower_tri_bf16(c)
    t_idx = lax.broadcasted_iota(jnp.int32, (c, c), 0)
    s_idx = lax.broadcasted_iota(jnp.int32, (c, c), 1)
    x = t_idx ^ s_idx
    split_level = jnp.full((c, c), -1, jnp.int32)
    for lvl in range(levels):
        split_level = split_level + (x >= (1 << lvl)).astype(jnp.int32)
    split_level = jnp.where(t_idx > s_idx, split_level, -1)
    row_idx = lax.broadcasted_iota(jnp.int32, (c, HEAD_DIM), 0)

    def head_chunk(r0, hh):
        cols = slice(hh * HEAD_DIM, (hh + 1) * HEAD_DIM)
        lb = lb_ref[:, cols]
        zf = f_ref[pl.ds(r0, c), cols].astype(F32)
        forget = lb + (1.0 - lb) * jax.nn.sigmoid(zf)
        log_f = jnp.log(jnp.maximum(forget, MIN_FORGET))
        key = (1.0 - lb) * jax.nn.sigmoid(-zf)
        hq = q_ref[pl.ds(r0, c), cols].astype(F32)
        query = hq * jax.nn.sigmoid(hq) * (HEAD_DIM ** -0.5)
        val = i_ref[pl.ds(r0, c), cols]
        cum = _tri_cumsum(tri, log_f)

        st = st_ref[hh]
        o = lax.dot_general((query * jnp.exp(cum)).astype(BF16), st.astype(BF16), NT_DIMS,
                            preferred_element_type=F32)

        scores = jnp.zeros((c, c), F32)
        seg_end = cum
        for lvl in range(levels):
            h = 1 << lvl
            odd = ((row_idx >> lvl) & 1) == 1
            prev_end = pltpu.roll(seg_end, h, axis=0)
            w = jnp.where(odd, query, key) * jnp.exp(jnp.where(odd, cum - prev_end, seg_end - cum))
            w = w.astype(BF16)
            part = lax.dot_general(w, w, NT_DIMS, preferred_element_type=F32)
            scores = scores + jnp.where(split_level == lvl, part, 0.0)
            seg_end = jnp.where(odd, seg_end, pltpu.roll(seg_end, c - h, axis=0))
        o = o + _dot(scores.astype(BF16), val)
        o = o + jnp.sum(query * key, axis=1, keepdims=True) * val.astype(F32)

        cum_last = cum[c - 1:c, :]
        k_dec = (key * jnp.exp(cum_last - cum)).astype(BF16)
        st_ref[hh] = st * jnp.exp(cum_last) + lax.dot_general(val, k_dec, TN_DIMS, preferred_element_type=F32)

        o = _rms(o, gain_ref[:, cols])
        hg = g_ref[pl.ds(r0, c), cols].astype(F32)
        o_ref[pl.ds(r0, c), cols] = (o * (hg * jax.nn.sigmoid(hg))).astype(o_ref.dtype)

    def step(ci, carry):
        r0 = pl.multiple_of(ci * c, c)
        for hh in range(group):
            head_chunk(r0, hh)
        return carry

    lax.fori_loop(0, rows // c, step, 0)


def _hgrn(z, heads, q_col, f_col, i_col, g_col, lower_bound, gain):
    s = z.shape[0]
    tm = _tile(s, ROWS["hgrn"])
    group =HGRN_HEADS_PER_STEP if heads % HGRN_HEADS_PER_STEP == 0 else 1
    width = group * HEAD_DIM
    assert all(c % width == 0 for c in (q_col, f_col, i_col, g_col))
    col = lambda c0: pl.BlockSpec((tm, width), lambda h, i: (i, c0 // width + h))
    vec = pl.BlockSpec((1, width), lambda h, i: (0, h))
    return pl.pallas_call(
        functools.partial(_hgrn_kernel, group=group),
        out_shape=jax.ShapeDtypeStruct((s, heads * HEAD_DIM), BF16),
        grid=(heads // group, s // tm),
        in_specs=[col(q_col), col(f_col), col(i_col), col(g_col), vec, vec],
        out_specs=pl.BlockSpec((tm, width), lambda h, i: (i, h)),
        scratch_shapes=[pltpu.VMEM((group, HEAD_DIM, HEAD_DIM), F32)],
        compiler_params=_params("parallel", "arbitrary"),
        name="hgrn2",
    )(z, z, z, z, lower_bound.reshape(1, -1), gain.reshape(1, -1))


def _merge_kernel(a_ref, b_ref, c_ref, wa_ref, wb_ref, wc_ref, ga_ref, gb_ref, gc_ref, o_ref):
    gate = lambda ref: jax.nn.sigmoid(ref[...].astype(F32))
    merged = gate(ga_ref) * _dot(a_ref[...], wa_ref[...])
    merged = merged + gate(gb_ref) * _dot(b_ref[...], wb_ref[...])
    merged = merged + gate(gc_ref) * _dot(c_ref[...], wc_ref[...])
    o_ref[...] = merged.astype(o_ref.dtype)


def _merge(att, conv, hgrn, wa, wb, wc, z, gate_col):
    s = att.shape[0]
    d = wa.shape[1]
    tm, tn = _tile(s, ROWS["merge"]), _tile(d, 512)
    assert gate_col % tn == 0 and d % tn == 0
    act = lambda a: pl.BlockSpec((tm, a.shape[1]), lambda i, j: (i, 0))
    wgt = lambda w: pl.BlockSpec((w.shape[0], tn), lambda i, j: (0, j))
    gate = lambda n: pl.BlockSpec((tm, tn), lambda i, j: (i, (gate_col + n * d) // tn + j))
    return pl.pallas_call(
        _merge_kernel,
        out_shape=jax.ShapeDtypeStruct((s, d), BF16),
        grid=(s // tm, d // tn),
        in_specs=[act(att), act(conv), act(hgrn), wgt(wa), wgt(wb), wgt(wc), gate(0), gate(1), gate(2)],
        out_specs=pl.BlockSpec((tm, tn), lambda i, j: (i, j)),
        compiler_params=_params("parallel", "parallel"),
        name="branch_merge",
    )(att, conv, hgrn, wa, wb, wc, z, z, z)


FFN_ROW_CHUNK = 64


def _ffn_up_kernel(h_ref, wg_ref, wu_ref, dwg_ref, dwu_ref, o_ref, carry_ref, ext_ref, lhs_ref, *, taps, nj, rc):
    i = pl.program_id(0)
    j = pl.program_id(1)
    tm = h_ref.shape[0]
    halo = SUBLANES
    jp = jnp.maximum(j - 1, 0)

    @pl.when((i == 0) & (j < nj))
    def _():
        carry_ref[j] = jnp.zeros(carry_ref.shape[1:], F32)

    @pl.when(j == 0)
    def _():
        lhs_ref[...] = h_ref[...]

    def finish_halo(slot):
        for n in range(2):
            ext_ref[slot, n, 0:halo, :] = carry_ref[jp, n]
            carry_ref[jp, n] = ext_ref[slot, n, tm:tm + halo, :]

    def finish(slot, r):
        lo = halo - (taps - 1) + r * rc
        conv = []
        for n, dw_ref in enumerate((dwg_ref, dwu_ref)):
            acc = ext_ref[slot, n, lo:lo + rc, :] * dw_ref[0:1, :]
            for k in range(1, taps):
                acc = acc + ext_ref[slot, n, lo + k:lo + k + rc, :] * dw_ref[k:k + 1, :]
            conv.append(acc)
        gate, up = conv
        out = (gate * jax.nn.sigmoid(gate) * up).astype(o_ref.dtype)
        o_ref[r * rc:(r + 1) * rc, :] = out
        return out

    def pace(r, out):
        nchunks = tm // rc
        bits = pltpu.bitcast(out[0:2 * SUBLANES, 0:LANES], jnp.uint32)
        zero = pltpu.bitcast((bits >> 16) >> 16, BF16)
        row0 = (tm // nchunks) * r
        col0 = ((lhs_ref.shape[1] // LANES) * r // nchunks) * LANES
        rows, cols = slice(row0, row0 + 2 * SUBLANES), slice(col0, col0 + LANES)
        lhs_ref[rows, cols] = lhs_ref[rows, cols] + zero

    def step(project_slot, finish_slot):
        if finish_slot is not None:
            finish_halo(finish_slot)
            for r in range(tm // rc):
                out = finish(finish_slot, r)
                if project_slot is not None:
                    pace(r, out)
        if project_slot is not None:
            ext_ref[project_slot, 0, halo:halo + tm, :] = _dot(lhs_ref[...], wg_ref[...])
            ext_ref[project_slot, 1, halo:halo + tm, :] = _dot(lhs_ref[...], wu_ref[...])

    interior = (j > 0) & (j < nj)
    pl.when(j == 0)(lambda: step(0, None))
    pl.when(interior & (j % 2 == 0))(lambda: step(0, 1))
    pl.when(interior & (j % 2 == 1))(lambda: step(1, 0))
    pl.when(j == nj)(lambda: step(None, (nj - 1) % 2))


def _ffn_up(h, w_up, dw, d_ff):
    s, d = h.shape
    taps = dw.shape[0]
    assert taps - 1 <= SUBLANES
    tm, tn = _tile(s, ROWS["ffn_up"]), _tile(d_ff, 256)
    nj = d_ff // tn
    cur = lambda j: jnp.minimum(j, nj - 1)
    fin = lambda j: jnp.maximum(j - 1, 0)
    return pl.pallas_call(
        functools.partial(_ffn_up_kernel, taps=taps, nj=nj, rc=_tile(tm, FFN_ROW_CHUNK)),
        out_shape=jax.ShapeDtypeStruct((s, d_ff), BF16),
        grid=(s // tm, nj + 1),
        in_specs=[
            pl.BlockSpec((tm, d), lambda i, j: (i, 0)),
            pl.BlockSpec((d, tn), lambda i, j: (0, cur(j))),
            pl.BlockSpec((d, tn), lambda i, j: (0, nj + cur(j))),
            pl.BlockSpec((taps, tn), lambda i, j: (0, fin(j))),
            pl.BlockSpec((taps, tn), lambda i, j: (0, nj + fin(j))),
        ],
        out_specs=pl.BlockSpec((tm, tn), lambda i, j: (i, fin(j))),
        scratch_shapes=[pltpu.VMEM((nj, 2, SUBLANES, tn), F32), pltpu.VMEM((2, 2, SUBLANES + tm, tn), F32),
                        pltpu.VMEM((tm, d), BF16)],
        compiler_params=_params("arbitrary", "arbitrary"),
        name="ffn_up_conv_glu",
    )(h, w_up, w_up, dw, dw)


def _layer(x, h, layer, p, w, g_next):
    heads = p["b_fgate"].shape[0]
    attn_w = heads * HEAD_DIM
    conv_c = p["conv_b"].shape[0]
    hgrn_w = p["hgrn_norm_g"].shape[0]
    d_ff = w["w_ffn_down"].shape[1]
    cast = lambda name: _cast_bf16(w[name], layer)

    fg0 = 3 * attn_w
    rest = w["w_in"].shape[-1] - LANES - fg0
    zq = _matmul(h, w["w_in"], BF16, 1024, 1024, "in_proj_qkv", layer, 0, fg0)
    z = _matmul(h, w["w_in"], BF16, 1024, 1024, "in_proj_rest", layer, fg0, rest)
    fg = _matmul(h, w["w_in"], F32, 1024, LANES, "in_proj_fgate", layer, fg0 + rest, LANES)
    hgrn_col = 2 * conv_c
    gate_col = hgrn_col + 4 * hgrn_w

    bias_row = jnp.pad(p["b_fgate"].astype(F32), (0, LANES - heads)).reshape(1, LANES)
    kx = _fgate_bias(fg, bias_row, heads)
    att = _attention(zq, kx, heads, 0, heads, 2 * heads)

    conv = _conv_module(z, 0, conv_c, p["conv_dw"], p["conv_b"], p["conv_ln_g"], p["conv_ln_b"])

    hg = _hgrn(z, hgrn_w // HEAD_DIM, hgrn_col, hgrn_col + hgrn_w, hgrn_col + 2 * hgrn_w, hgrn_col + 3 * hgrn_w,
               p["lower_bound"], p["hgrn_norm_g"])

    merged = _merge(att, conv, hg, cast("w_attn_out"), cast("w_conv_out"), cast("w_hgrn_out"), z, gate_col)
    y = _matmul(merged, cast("w_o"), F32, 1024, 1024, "out_proj")
    x, h = _residual_norm(x, y, p["norm_gains"][1], p["norm_gains"][2])

    act = _ffn_up(h, cast("w_ffn_up"), p["ffn_dw"], d_ff)
    y = _matmul(act, cast("w_ffn_down"), F32, 512, 512, "ffn_down")
    return _residual_norm(x, y, p["norm_gains"][3], g_next)


def kernel(x, norm_gains, w_in, b_fgate, conv_dw, conv_b, conv_ln_g, conv_ln_b, hgrn_lb_logits, hgrn_norm_g,
           w_attn_out, w_conv_out, w_hgrn_out, w_o, w_ffn_up, ffn_dw, w_ffn_down):
    b, s, d = x.shape
    assert b == 1, "the sequence scans assume a single sequence"
    depth = w_in.shape[0]
    p_lb = jax.nn.softmax(hgrn_lb_logits.astype(F32), axis=0)
    lower_bounds = jnp.cumsum(p_lb, axis=0) - p_lb[0]
    weights = dict(w_in=_reorder_w_in(w_in, b_fgate.shape[1]), w_attn_out=w_attn_out, w_conv_out=w_conv_out,
                   w_hgrn_out=w_hgrn_out, w_o=w_o, w_ffn_up=w_ffn_up, w_ffn_down=w_ffn_down)

    xs = x.reshape(s, d)
    h = _rmsnorm_bf16(xs, norm_gains[0, 0])
    for l in range(depth):
        p = dict(norm_gains=norm_gains[l], b_fgate=b_fgate[l], conv_dw=conv_dw[l], conv_b=conv_b[l],
                 conv_ln_g=conv_ln_g[l], conv_ln_b=conv_ln_b[l], lower_bound=lower_bounds[l],
                 hgrn_norm_g=hgrn_norm_g[l], ffn_dw=ffn_dw[l])
        g_next = norm_gains[l + 1, 0] if l + 1 < depth else None
        xs, h = _layer(xs, h, l, p, weights, g_next)
    return xs.reshape(b, s, d)
```
